```python
import math
import numpy as np
import jax
import jax.numpy as jnp
from jax import lax

D_MODEL = 1024
BATCH = 2
SEQ = 8192
DEPTH = 1
DEC_BATCH = 128
DEC_SEQ = 8
PAST_LEN = 16384
PAGE_SIZE = 128

MOBA_HEADS = 8
MOBA_KV_HEADS = 4
MOBA_HEAD_DIM = 64
MOBA_BLOCK = 256
MOBA_TOPK = 3
MLA_HEADS = 8
MLA_NOPE = 64
MLA_ROPE = 32
MLA_V = 64
MLA_Q_LORA = 256
MLA_KV_LORA = 256
ROPE_THETA = 10000.0
REL_BUCKETS = 32
REL_MAX_DIST = 128
PEER_HEADS = 8
PEER_NKEYS = 128
PEER_EXPERTS = PEER_NKEYS * PEER_NKEYS
PEER_KEY_DIM = 128
PEER_TOPK = 16
PEER_BLOCK = 128
Q_BLOCK = 128
N_BRANCHES = 2
LN_EPS = 1e-5
RMS_EPS = 1e-6
NEG_INF = -1e30
DEEPNORM_ALPHA = (2 * DEPTH) ** 0.25
DEEPNORM_BETA = (8 * DEPTH) ** -0.25
IN_SPLITS = (MOBA_HEADS * MOBA_HEAD_DIM, MOBA_KV_HEADS * MOBA_HEAD_DIM, MOBA_KV_HEADS * MOBA_HEAD_DIM,
             MLA_Q_LORA, MLA_KV_LORA, MLA_ROPE, N_BRANCHES * D_MODEL)
IN_COLS = sum(IN_SPLITS)

kernel_name = 'hybrid_moba_mla_peer_step'


def layer_norm(x, g, b):
    xf = x.astype(jnp.float32)
    mu = jnp.mean(xf, -1, keepdims=True)
    var = jnp.mean(jnp.square(xf - mu), -1, keepdims=True)
    y = (xf - mu) * lax.rsqrt(var + LN_EPS) * g.astype(jnp.float32) + b.astype(jnp.float32)
    return y.astype(x.dtype)


def rms_norm(x, g):
    xf = x.astype(jnp.float32)
    y = xf * lax.rsqrt(jnp.mean(xf * xf, -1, keepdims=True) + RMS_EPS) * g.astype(jnp.float32)
    return y.astype(x.dtype)


def rope(x, pos):
    half = x.shape[-1] // 2
    inv_freq = ROPE_THETA ** (-jnp.arange(half, dtype=jnp.float32) / half)
    ang = pos.astype(jnp.float32)[:, None] * inv_freq[None, :]
    cos = jnp.cos(ang)[:, None, :]
    sin = jnp.sin(ang)[:, None, :]
    xf = x.astype(jnp.float32)
    x1, x2 = xf[..., :half], xf[..., half:]
    return jnp.concatenate([x1 * cos - x2 * sin, x1 * sin + x2 * cos], -1).astype(x.dtype)


def rel_bucket(n):
    n = jnp.maximum(n, 0)
    max_exact = REL_BUCKETS // 2
    nf = jnp.maximum(n, 1).astype(jnp.float32)
    large = max_exact + (jnp.log(nf / max_exact) / math.log(REL_MAX_DIST / max_exact)
                         * (REL_BUCKETS - max_exact)).astype(jnp.int32)
    return jnp.where(n < max_exact, n, jnp.minimum(large, REL_BUCKETS - 1))


def kv_head_of_q():
    return jnp.arange(MOBA_HEADS) // (MOBA_HEADS // MOBA_KV_HEADS)


def block_positions(idx):
    return (idx[..., None] * MOBA_BLOCK + jnp.arange(MOBA_BLOCK)).reshape(idx.shape[:3] + (-1,))


def project(x, pos, w_in, b_gate, mla_q_norm, mla_w_uq, mla_kv_norm, mla_w_uk):
    b, s = x.shape[:2]
    h = jnp.einsum('bsd,dc->bsc', x, w_in)
    qm, km, vm, cq, ckv, kr, gates = jnp.split(h, np.cumsum(IN_SPLITS)[:-1].tolist(), axis=-1)
    qm = qm.reshape(b, s, MOBA_HEADS, MOBA_HEAD_DIM)
    km = km.reshape(b, s, MOBA_KV_HEADS, MOBA_HEAD_DIM)
    vm = vm.reshape(b, s, MOBA_KV_HEADS, MOBA_HEAD_DIM)
    q_mla = jnp.einsum('bsc,chd->bshd', rms_norm(cq, mla_q_norm), mla_w_uq)
    q_nope, q_pe = q_mla[..., :MLA_NOPE], rope(q_mla[..., MLA_NOPE:], pos)
    q_lat = jnp.einsum('bshn,chn->bshc', q_nope, mla_w_uk)
    ckv = rms_norm(ckv, mla_kv_norm)
    kpe = rope(kr[:, :, None, :], pos)[:, :, 0, :]
    gates = jax.nn.sigmoid((gates + b_gate).astype(jnp.float32)).astype(x.dtype)
    return qm, km, vm, q_lat, q_pe, ckv, kpe, gates


def moba_select(q, pos, kmean, n_sel):
    kvh = kv_head_of_q()
    s = jnp.einsum('bqhd,bnhd->bhqn', q.astype(jnp.float32), kmean[:, :, kvh].astype(jnp.float32))
    own_blk = pos // MOBA_BLOCK
    s = jnp.where(jnp.arange(kmean.shape[1])[None, :] < own_blk[:, None], s, NEG_INF)
    _, idx = lax.top_k(s, n_sel)
    valid = jnp.arange(n_sel)[None, :] < own_blk[:, None]
    return idx, valid


def moba_core(q, pos, sel, own_k, own_v, own_pos, rel_bias):
    kvh = kv_head_of_q()
    scale = MOBA_HEAD_DIM ** -0.5
    bias_t = rel_bias.T.astype(jnp.float32)
    qf = q.astype(jnp.float32)
    own_kq = own_k[:, :, kvh].astype(jnp.float32)
    own_vq = own_v[:, :, kvh].astype(jnp.float32)
    d_own = pos[:, None] - own_pos[None, :]
    s_own = jnp.einsum('bqhd,bkhd->bhqk', qf, own_kq) * scale + bias_t[:, rel_bucket(d_own)][None]
    s_own = jnp.where((d_own >= 0)[None, None], s_own, NEG_INF)
    if sel is None:
        p = jax.nn.softmax(s_own, -1)
        return jnp.einsum('bhqk,bkhd->bqhd', p, own_vq).astype(q.dtype)
    sel_k, sel_v, sel_pos, sel_valid = sel
    n_k = sel_k.shape[3]
    h_idx = jnp.arange(MOBA_HEADS)[None, :, None, None]
    s_sel = (jnp.einsum('bqhd,bhqkd->bhqk', qf, sel_k.astype(jnp.float32)) * scale
             + bias_t[h_idx, rel_bucket(pos[None, None, :, None] - sel_pos)])
    s_sel = jnp.where(sel_valid, s_sel, NEG_INF)
    p = jax.nn.softmax(jnp.concatenate([s_sel, s_own], -1), -1)
    out = (jnp.einsum('bhqk,bhqkd->bqhd', p[..., :n_k], sel_v.astype(jnp.float32))
           + jnp.einsum('bhqk,bkhd->bqhd', p[..., n_k:], own_vq))
    return out.astype(q.dtype)


def mla_core(q_lat, q_pe, pos, ckv, kpe, kpos):
    scale = (MLA_NOPE + MLA_ROPE) ** -0.5
    ckv_f = ckv.astype(jnp.float32)
    s = (jnp.einsum('bqhc,bkc->bhqk', q_lat.astype(jnp.float32), ckv_f)
         + jnp.einsum('bqhr,bkr->bhqk', q_pe.astype(jnp.float32), kpe.astype(jnp.float32))) * scale
    s = jnp.where(kpos[None, :] <= pos[:, None], s, NEG_INF)
    p = jax.nn.softmax(s, -1)
    return jnp.einsum('bhqk,bkc->bqhc', p, ckv_f).astype(q_lat.dtype)


def prompt_attention(qm, km, vm, q_lat, q_pe, ckv, kpe, rel_bias):
    b, s = qm.shape[:2]
    n_blk = s // MOBA_BLOCK
    n_sel = min(MOBA_TOPK, n_blk)
    kvh = kv_head_of_q()
    if n_sel > 0:
        kb = km[:, :n_blk * MOBA_BLOCK].reshape(b, n_blk, MOBA_BLOCK, MOBA_KV_HEADS, MOBA_HEAD_DIM)
        vb = vm[:, :n_blk * MOBA_BLOCK].reshape(b, n_blk, MOBA_BLOCK, MOBA_KV_HEADS, MOBA_HEAD_DIM)
        kmean = jnp.mean(kb, axis=2, dtype=jnp.float32)
        kb = kb.transpose(0, 3, 1, 2, 4)
        vb = vb.transpose(0, 3, 1, 2, 4)
        b_idx = jnp.arange(b)[:, None, None, None]
        h_idx = kvh[None, :, None, None]
    pad = ((0, 0), (0, MOBA_BLOCK), (0, 0), (0, 0))
    k_pad = jnp.pad(km, pad)
    v_pad = jnp.pad(vm, pad)
    kpos = jnp.arange(s)

    def chunk(c):
        start = c * Q_BLOCK
        pos = start + jnp.arange(Q_BLOCK)
        q_c = lax.dynamic_slice_in_dim(qm, start, Q_BLOCK, axis=1)
        own_start = (start // MOBA_BLOCK) * MOBA_BLOCK
        own_k = lax.dynamic_slice_in_dim(k_pad, own_start, MOBA_BLOCK, axis=1)
        own_v = lax.dynamic_slice_in_dim(v_pad, own_start, MOBA_BLOCK, axis=1)
        own_pos = own_start + jnp.arange(MOBA_BLOCK)
        sel = None
        if n_sel > 0:
            idx, valid = moba_select(q_c, pos, kmean, n_sel)
            sk = kb[b_idx, h_idx, idx].reshape(b, MOBA_HEADS, Q_BLOCK, n_sel * MOBA_BLOCK, MOBA_HEAD_DIM)
            sv = vb[b_idx, h_idx, idx].reshape(b, MOBA_HEADS, Q_BLOCK, n_sel * MOBA_BLOCK, MOBA_HEAD_DIM)
            sel = (sk, sv, block_positions(idx), jnp.repeat(valid, MOBA_BLOCK, axis=-1)[None, None])
        o_moba = moba_core(q_c, pos, sel, own_k, own_v, own_pos, rel_bias)
        o_mla = mla_core(lax.dynamic_slice_in_dim(q_lat, start, Q_BLOCK, axis=1),
                         lax.dynamic_slice_in_dim(q_pe, start, Q_BLOCK, axis=1), pos, ckv, kpe, kpos)
        return o_moba, o_mla

    o_moba, o_mla = lax.map(chunk, jnp.arange(s // Q_BLOCK))
    o_moba = jnp.swapaxes(o_moba, 0, 1).reshape(b, s, MOBA_HEADS, MOBA_HEAD_DIM)
    o_mla = jnp.swapaxes(o_mla, 0, 1).reshape(b, s, MLA_HEADS, MLA_KV_LORA)
    return o_moba, o_mla


def sample_attention(qm, km, vm, q_lat, q_pe, ckv, kpe, cache_moba_k, cache_moba_v, cache_mla_ckv,
                     cache_mla_kpe, page_kmean, page_table, layer, rel_bias):
    n_seq, ds = qm.shape[:2]
    n_pages = page_table.shape[1]
    past = n_pages * PAGE_SIZE
    ppb = MOBA_BLOCK // PAGE_SIZE
    n_blk = past // MOBA_BLOCK
    n_sel = min(MOBA_TOPK, n_blk)
    rem_pages = (past % MOBA_BLOCK) // PAGE_SIZE
    rem = rem_pages * PAGE_SIZE
    pos = past + jnp.arange(ds)
    kpos = jnp.arange(past + ds)
    own_pos = past - rem + jnp.arange(rem + ds)
    kvh = kv_head_of_q()
    blk_mean = page_kmean[layer][page_table[:, :n_blk * ppb]].reshape(
        n_seq, n_blk, ppb, MOBA_KV_HEADS, MOBA_HEAD_DIM).mean(axis=2)

    def one_seq(args):
        qm_s, km_s, vm_s, ql_s, qp_s, ckv_s, kpe_s, pt, bm = args
        own_k, own_v = km_s, vm_s
        if rem_pages > 0:
            tail = pt[n_pages - rem_pages:]
            past_k = cache_moba_k[layer, tail].transpose(0, 2, 1, 3).reshape(rem, MOBA_KV_HEADS, MOBA_HEAD_DIM)
            past_v = cache_moba_v[layer, tail].transpose(0, 2, 1, 3).reshape(rem, MOBA_KV_HEADS, MOBA_HEAD_DIM)
            own_k = jnp.concatenate([past_k, km_s], axis=0)
            own_v = jnp.concatenate([past_v, vm_s], axis=0)
        sel = None
        if n_sel > 0:
            idx, valid = moba_select(qm_s[None], pos, bm[None], n_sel)
            phys = pt[idx[..., None] * ppb + jnp.arange(ppb)]
            h_idx = kvh[None, :, None, None, None]
            sk = cache_moba_k[layer, phys, h_idx].reshape(1, MOBA_HEADS, ds, n_sel * MOBA_BLOCK, MOBA_HEAD_DIM)
            sv = cache_moba_v[layer, phys, h_idx].reshape(1, MOBA_HEADS, ds, n_sel * MOBA_BLOCK, MOBA_HEAD_DIM)
            sel = (sk, sv, block_positions(idx), jnp.repeat(valid, MOBA_BLOCK, axis=-1)[None, None])
        o_moba = moba_core(qm_s[None], pos, sel, own_k[None], own_v[None], own_pos, rel_bias)
        ckv_all = jnp.concatenate([cache_mla_ckv[layer, pt].reshape(past, MLA_KV_LORA), ckv_s], axis=0)
        kpe_all = jnp.concatenate([cache_mla_kpe[layer, pt].reshape(past, MLA_ROPE), kpe_s], axis=0)
        o_mla = mla_core(ql_s[None], qp_s[None], pos, ckv_all[None], kpe_all[None], kpos)
        return o_moba[0], o_mla[0]

    return lax.map(one_seq, (qm, km, vm, q_lat, q_pe, ckv, kpe, page_table, blk_mean))


def peer_ffn(x2d, w_q, k1, k2, u_tab, v_tab):
    n = x2d.shape[0]
    n_blk = -(-n // PEER_BLOCK)
    xp = jnp.pad(x2d, ((0, n_blk * PEER_BLOCK - n), (0, 0))).reshape(n_blk, PEER_BLOCK, D_MODEL)
    half = PEER_KEY_DIM // 2

    def block(xc):
        q = jnp.einsum('td,dc->tc', xc, w_q).reshape(PEER_BLOCK, PEER_HEADS, PEER_KEY_DIM).astype(jnp.float32)
        s1 = jnp.einsum('thc,hnc->thn', q[..., :half], k1.astype(jnp.float32))
        s2 = jnp.einsum('thc,hnc->thn', q[..., half:], k2.astype(jnp.float32))
        v1, i1 = lax.top_k(s1, PEER_TOPK)
        v2, i2 = lax.top_k(s2, PEER_TOPK)
        cand = (v1[..., :, None] + v2[..., None, :]).reshape(PEER_BLOCK, PEER_HEADS, PEER_TOPK * PEER_TOPK)
        sc, ci = lax.top_k(cand, PEER_TOPK)
        e = (jnp.take_along_axis(i1, ci // PEER_TOPK, -1) * PEER_NKEYS
             + jnp.take_along_axis(i2, ci % PEER_TOPK, -1))
        g = jax.nn.softmax(sc, -1)
        act = jax.nn.gelu(jnp.einsum('td,thkd->thk', xc, u_tab[e]).astype(jnp.float32), approximate=False)
        return jnp.einsum('thk,thkd->td', (g * act).astype(xc.dtype), v_tab[e])

    return lax.map(block, xp).reshape(n_blk * PEER_BLOCK, D_MODEL)[:n]


def finish_layer(x, o_moba, o_mla, gates, w_uv, w_br_moba, w_br_mla, w_out, ln1_g, ln1_b,
                 peer_wq, peer_k1, peer_k2, peer_u, peer_v, ln2_g, ln2_b):
    b, s = x.shape[:2]
    mla_o = jnp.einsum('bshc,chv->bshv', o_mla, w_uv).reshape(b, s, MLA_HEADS * MLA_V)
    y_moba = jnp.einsum('bsc,cd->bsd', o_moba.reshape(b, s, MOBA_HEADS * MOBA_HEAD_DIM), w_br_moba)
    y_mla = jnp.einsum('bsc,cd->bsd', mla_o, w_br_mla)
    merged = gates[..., :D_MODEL] * y_moba + gates[..., D_MODEL:] * y_mla
    mix = jnp.einsum('bsd,de->bse', merged, w_out)
    h = layer_norm(DEEPNORM_ALPHA * x + mix, ln1_g, ln1_b)
    f = peer_ffn(h.reshape(b * s, D_MODEL), peer_wq, peer_k1, peer_k2, peer_u, peer_v).reshape(b, s, D_MODEL)
    return layer_norm(DEEPNORM_ALPHA * h + f, ln2_g, ln2_b)


def setup_inputs(seed: int = 0) -> dict:
    key = jax.random.key(seed)
    ks = jax.random.split(key, 32)
    f32 = jnp.float32
    n_pages = PAST_LEN // PAGE_SIZE
    n_used = DEC_BATCH * n_pages
    n_pool = n_used + max(1, n_used // 4)
    moba_w = MOBA_HEADS * MOBA_HEAD_DIM
    mla_w = MLA_HEADS * MLA_V

    def nrm(k, shape, scale=1.0):
        return jax.random.normal(k, shape, f32) * scale

    return {
        'x_prompt': nrm(ks[0], (BATCH, SEQ, D_MODEL)),
        'x_sample': nrm(ks[1], (DEC_BATCH, DEC_SEQ, D_MODEL)),
        'cache_moba_k': nrm(ks[2], (DEPTH, n_pool, MOBA_KV_HEADS, PAGE_SIZE, MOBA_HEAD_DIM)),
        'cache_moba_v': nrm(ks[3], (DEPTH, n_pool, MOBA_KV_HEADS, PAGE_SIZE, MOBA_HEAD_DIM)),
        'cache_mla_ckv': nrm(ks[4], (DEPTH, n_pool, PAGE_SIZE, MLA_KV_LORA)),
        'cache_mla_kpe': nrm(ks[5], (DEPTH, n_pool, PAGE_SIZE, MLA_ROPE)),
        'page_table': jax.random.permutation(ks[6], n_pool)[:n_used].reshape(DEC_BATCH, n_pages).astype(jnp.int32),
        'w_in': nrm(ks[7], (DEPTH, D_MODEL, IN_COLS), D_MODEL ** -0.5),
        'b_gate': nrm(ks[8], (DEPTH, N_BRANCHES * D_MODEL), 0.01),
        'mla_q_norm': 1.0 + nrm(ks[9], (DEPTH, MLA_Q_LORA), 0.01),
        'mla_w_uq': nrm(ks[10], (DEPTH, MLA_Q_LORA, MLA_HEADS, MLA_NOPE + MLA_ROPE), MLA_Q_LORA ** -0.5),
        'mla_kv_norm': 1.0 + nrm(ks[11], (DEPTH, MLA_KV_LORA), 0.01),
        'mla_w_uk': nrm(ks[12], (DEPTH, MLA_KV_LORA, MLA_HEADS, MLA_NOPE), MLA_KV_LORA ** -0.5),
        'mla_w_uv': nrm(ks[13], (DEPTH, MLA_KV_LORA, MLA_HEADS, MLA_V), MLA_KV_LORA ** -0.5),
        'rel_bias': nrm(ks[14], (REL_BUCKETS, MOBA_HEADS), 0.5),
        'w_br_moba': nrm(ks[15], (DEPTH, moba_w, D_MODEL), moba_w ** -0.5 * DEEPNORM_BETA),
        'w_br_mla': nrm(ks[16], (DEPTH, mla_w, D_MODEL), mla_w ** -0.5 * DEEPNORM_BETA),
        'w_out': nrm(ks[17], (DEPTH, D_MODEL, D_MODEL), D_MODEL ** -0.5 * DEEPNORM_BETA),
        'ln1_g': 1.0 + nrm(ks[18], (DEPTH, D_MODEL), 0.01),
        'ln1_b': nrm(ks[19], (DEPTH, D_MODEL), 0.01),
        'peer_wq': nrm(ks[20], (DEPTH, D_MODEL, PEER_HEADS * PEER_KEY_DIM), D_MODEL ** -0.5),
        'peer_k1': nrm(ks[21], (DEPTH, PEER_HEADS, PEER_NKEYS, PEER_KEY_DIM // 2), (PEER_KEY_DIM // 2) ** -0.5),
        'peer_k2': nrm(ks[22], (DEPTH, PEER_HEADS, PEER_NKEYS, PEER_KEY_DIM // 2), (PEER_KEY_DIM // 2) ** -0.5),
        'peer_u': nrm(ks[23], (DEPTH, PEER_EXPERTS, D_MODEL), D_MODEL ** -0.5),
        'peer_v': nrm(ks[24], (DEPTH, PEER_EXPERTS, D_MODEL), 0.5 * DEEPNORM_BETA),
        'ln2_g': 1.0 + nrm(ks[25], (DEPTH, D_MODEL), 0.01),
        'ln2_b': nrm(ks[26], (DEPTH, D_MODEL), 0.01),
    }


def reference(x_prompt, x_sample, cache_moba_k, cache_moba_v, cache_mla_ckv, cache_mla_kpe, page_table,
              w_in, b_gate, mla_q_norm, mla_w_uq, mla_kv_norm, mla_w_uk, mla_w_uv, rel_bias,
              w_br_moba, w_br_mla, w_out, ln1_g, ln1_b, peer_wq, peer_k1, peer_k2, peer_u, peer_v,
              ln2_g, ln2_b):
    past = page_table.shape[1] * PAGE_SIZE
    pos_p = jnp.arange(x_prompt.shape[1])
    pos_s = past + jnp.arange(x_sample.shape[1])
    page_kmean = jnp.mean(cache_moba_k, axis=3, dtype=jnp.float32)
    h_p, h_s = x_prompt, x_sample
    kp, vp, cp, rp, ksm, vsm, csm, rsm = [], [], [], [], [], [], [], []
    for layer in range(DEPTH):
        proj_w = (w_in[layer], b_gate[layer], mla_q_norm[layer], mla_w_uq[layer], mla_kv_norm[layer], mla_w_uk[layer])
        post_w = (mla_w_uv[layer], w_br_moba[layer], w_br_mla[layer], w_out[layer], ln1_g[layer], ln1_b[layer],
                  peer_wq[layer], peer_k1[layer], peer_k2[layer], peer_u[layer], peer_v[layer],
                  ln2_g[layer], ln2_b[layer])
        qm, km, vm, ql, qp, ckv, kpe, g = project(h_p, pos_p, *proj_w)
        o_moba, o_mla = prompt_attention(qm, km, vm, ql, qp, ckv, kpe, rel_bias)
        h_p = finish_layer(h_p, o_moba, o_mla, g, *post_w)
        kp.append(km)
        vp.append(vm)
        cp.append(ckv)
        rp.append(kpe)
        qm, km, vm, ql, qp, ckv, kpe, g = project(h_s, pos_s, *proj_w)
        o_moba, o_mla = sample_attention(qm, km, vm, ql, qp, ckv, kpe, cache_moba_k, cache_moba_v,
                                         cache_mla_ckv, cache_mla_kpe, page_kmean, page_table, layer, rel_bias)
        h_s = finish_layer(h_s, o_moba, o_mla, g, *post_w)
        ksm.append(km)
        vsm.append(vm)
        csm.append(ckv)
        rsm.append(kpe)
    return (h_p, h_s, jnp.stack(kp), jnp.stack(vp), jnp.stack(cp), jnp.stack(rp),
            jnp.stack(ksm), jnp.stack(vsm), jnp.stack(csm), jnp.stack(rsm))
```

```python
import functools
import math

import numpy as np
import jax
import jax.numpy as jnp
from jax import lax
from jax.experimental import pallas as pl
from jax.experimental.pallas import tpu as pltpu

D_MODEL = 1024
PAGE_SIZE = 128
MOBA_HEADS = 8
MOBA_KV_HEADS = 4
MOBA_GROUP = MOBA_HEADS // MOBA_KV_HEADS
MOBA_HEAD_DIM = 64
MOBA_BLOCK = 256
MOBA_TOPK = 3
MLA_HEADS = 8
MLA_NOPE = 64
MLA_ROPE = 32
MLA_V = 64
MLA_Q_LORA = 256
MLA_KV_LORA = 256
ROPE_THETA = 10000.0
REL_BUCKETS = 32
REL_MAX_DIST = 128
PEER_HEADS = 8
PEER_NKEYS = 128
PEER_KEY_DIM = 128
PEER_TOPK = 16
Q_BLOCK = 128
LN_EPS = 1e-5
RMS_EPS = 1e-6
NEG_INF = -1e30
KNOCKED_OUT = -3e38

VMEM_LIMIT_BYTES = 56 * 1024 * 1024

BF16 = jnp.bfloat16
F32 = jnp.float32


def _dot(a, b):
    return jnp.dot(a, b, preferred_element_type=F32)


def _dot_nt(a, b, precision=None):
    return lax.dot_general(a, b, (((1,), (1,)), ((), ())), precision=precision,
                           preferred_element_type=F32)


def _params(*sem):
    return pltpu.CompilerParams(dimension_semantics=sem, vmem_limit_bytes=VMEM_LIMIT_BYTES)


def _rms(x, g):
    return x * lax.rsqrt(jnp.mean(x * x, axis=-1, keepdims=True) + RMS_EPS) * g


def _layer_norm(x, g, b):
    mu = jnp.mean(x, axis=-1, keepdims=True)
    xc = x - mu
    var = jnp.mean(xc * xc, axis=-1, keepdims=True)
    return xc * lax.rsqrt(var + LN_EPS) * g + b


def _sigmoid(x):
    return 1.0 / (1.0 + jnp.exp(-x))


def _softmax_update(s, v, m_ref, l_ref, acc_ref):
    m_old = m_ref[...]
    m_new = jnp.maximum(m_old, jnp.max(s, axis=-1, keepdims=True))
    alpha = jnp.exp(m_old - m_new)
    p = jnp.exp(s - m_new)
    l_ref[...] = alpha * l_ref[...] + jnp.sum(p, axis=-1, keepdims=True)
    acc_ref[...] = alpha * acc_ref[...] + _dot(p.astype(BF16), v)
    m_ref[...] = m_new


def _softmax_init(m_ref, l_ref, acc_ref):
    m_ref[...] = jnp.full(m_ref.shape, NEG_INF, F32)
    l_ref[...] = jnp.zeros(l_ref.shape, F32)
    acc_ref[...] = jnp.zeros(acc_ref.shape, F32)


def _top_mask(s, n_sel):
    col = lax.broadcasted_iota(jnp.int32, s.shape, 1)
    sel = jnp.zeros(s.shape, F32)
    for _ in range(n_sel):
        m = jnp.max(s, axis=-1, keepdims=True)
        idx = jnp.min(jnp.where(s == m, col, s.shape[1]), axis=-1, keepdims=True)
        pick = col == idx
        sel = jnp.where(pick, 1.0, sel)
        s = jnp.where(pick, KNOCKED_OUT, s)
    return sel


def _proj_kernel(x_ref, tab_ref, wqkv_ref, wlat_ref, wgate_ref, bgate_ref, qn_ref, kvn_ref, wuq_ref,
                 wuk_ref, qm_ref, km_ref, vm_ref, kb_ref, vb_ref, kmean_ref, ckv_ref, ckvb_ref,
                 kpe_ref, kpeb_ref, gates_ref, qlat_ref, qpe_ref):
    xb = x_ref[...].astype(BF16)
    qkv = _dot(xb, wqkv_ref[...])
    moba_w = MOBA_HEADS * MOBA_HEAD_DIM
    kv_w = MOBA_KV_HEADS * MOBA_HEAD_DIM
    for h in range(MOBA_HEADS):
        qm_ref[h] = qkv[:, h * MOBA_HEAD_DIM:(h + 1) * MOBA_HEAD_DIM]
    km = qkv[:, moba_w:moba_w + kv_w]
    vm = qkv[:, moba_w + kv_w:moba_w + 2 * kv_w]
    km_ref[...] = km
    vm_ref[...] = vm
    for g in range(MOBA_KV_HEADS):
        kb_ref[g] = km[:, g * MOBA_HEAD_DIM:(g + 1) * MOBA_HEAD_DIM].astype(BF16)
        vb_ref[g] = vm[:, g * MOBA_HEAD_DIM:(g + 1) * MOBA_HEAD_DIM].astype(BF16)
    kmean_ref[0] = jnp.mean(km, axis=0, keepdims=True)

    lat = _dot(xb, wlat_ref[...])
    cq = lat[:, :MLA_Q_LORA]
    ckv = lat[:, MLA_Q_LORA:MLA_Q_LORA + MLA_KV_LORA]
    base = MLA_Q_LORA + MLA_KV_LORA
    kr = lat[:, base:base + MLA_ROPE]
    kr_swapped = lat[:, base + MLA_ROPE:base + 2 * MLA_ROPE]
    ckv_n = _rms(ckv, kvn_ref[...])
    ckv_ref[...] = ckv_n
    ckvb_ref[...] = ckv_n.astype(BF16)
    pe_w = MLA_HEADS * MLA_ROPE
    cos = tab_ref[:, :pe_w]
    sin = tab_ref[:, pe_w:]
    kpe = kr * cos[:, :MLA_ROPE] + kr_swapped * sin[:, :MLA_ROPE]
    kpe_ref[...] = kpe
    kpeb_ref[...] = kpe.astype(BF16)

    gates_ref[...] = _sigmoid(_dot(xb, wgate_ref[...]) + bgate_ref[...])

    scale = (MLA_NOPE + MLA_ROPE) ** -0.5
    cqn = _rms(cq, qn_ref[...]).astype(BF16)
    qq = _dot(cqn, wuq_ref[...])
    nope_w = MLA_HEADS * MLA_NOPE
    qpe = (qq[:, nope_w:nope_w + pe_w] * cos + qq[:, nope_w + pe_w:] * sin) * scale
    for h in range(MLA_HEADS):
        qpe_ref[h] = qpe[:, h * MLA_ROPE:(h + 1) * MLA_ROPE]
        q_nope = qq[:, h * MLA_NOPE:(h + 1) * MLA_NOPE].astype(BF16)
        qlat_ref[h] = _dot(q_nope, wuk_ref[h]) * scale


def _project(x_all, rope_tab, wqkv, wlat, wgate, bgate, qn, kvn, wuq, wuk):
    n = x_all.shape[0]
    tb = MOBA_BLOCK
    nb = n // tb
    row = lambda w: pl.BlockSpec((tb, w), lambda i: (i, 0))
    full = lambda a: pl.BlockSpec(a.shape, lambda i: (0,) * a.ndim)
    head = lambda nh, w: pl.BlockSpec((nh, tb, w), lambda i: (0, i, 0))
    kvw = MOBA_KV_HEADS * MOBA_HEAD_DIM
    out_shape = (
        jax.ShapeDtypeStruct((MOBA_HEADS, n, MOBA_HEAD_DIM), F32),
        jax.ShapeDtypeStruct((n, kvw), F32),
        jax.ShapeDtypeStruct((n, kvw), F32),
        jax.ShapeDtypeStruct((MOBA_KV_HEADS, n, MOBA_HEAD_DIM), BF16),
        jax.ShapeDtypeStruct((MOBA_KV_HEADS, n, MOBA_HEAD_DIM), BF16),
        jax.ShapeDtypeStruct((nb, 1, kvw), F32),
        jax.ShapeDtypeStruct((n, MLA_KV_LORA), F32),
        jax.ShapeDtypeStruct((n, MLA_KV_LORA), BF16),
        jax.ShapeDtypeStruct((n, MLA_ROPE), F32),
        jax.ShapeDtypeStruct((n, MLA_ROPE), BF16),
        jax.ShapeDtypeStruct((n, 2 * D_MODEL), F32),
        jax.ShapeDtypeStruct((MLA_HEADS, n, MLA_KV_LORA), F32),
        jax.ShapeDtypeStruct((MLA_HEADS, n, MLA_ROPE), F32),
    )
    out_specs = (
        head(MOBA_HEADS, MOBA_HEAD_DIM), row(kvw), row(kvw),
        head(MOBA_KV_HEADS, MOBA_HEAD_DIM), head(MOBA_KV_HEADS, MOBA_HEAD_DIM),
        pl.BlockSpec((1, 1, kvw), lambda i: (i, 0, 0)),
        row(MLA_KV_LORA), row(MLA_KV_LORA), row(MLA_ROPE), row(MLA_ROPE), row(2 * D_MODEL),
        head(MLA_HEADS, MLA_KV_LORA), head(MLA_HEADS, MLA_ROPE),
    )
    return pl.pallas_call(
        _proj_kernel,
        grid=(nb,),
        in_specs=[row(D_MODEL), row(rope_tab.shape[1]), full(wqkv), full(wlat), full(wgate), full(bgate),
                  full(qn), full(kvn), full(wuq), full(wuk)],
        out_specs=out_specs,
        out_shape=out_shape,
        compiler_params=_params("parallel"),
        name="in_projection",
    )(x_all, rope_tab, wqkv, wlat, wgate, bgate, qn, kvn, wuq, wuk)


def _prompt_mla_kernel(qlat_ref, qpe_ref, ckv_ref, kpe_ref, out_ref, m_ref, l_ref, acc_ref, *, tq, tk):
    c = pl.program_id(1)
    rows = MLA_HEADS * tq
    q1 = qlat_ref[...].reshape(rows, MLA_KV_LORA).astype(BF16)
    q2 = qpe_ref[...].reshape(rows, MLA_ROPE).astype(BF16)
    _softmax_init(m_ref, l_ref, acc_ref)
    q_pos = c * tq + jnp.bitwise_and(lax.broadcasted_iota(jnp.int32, (rows, tk), 0), tq - 1)
    k_off = lax.broadcasted_iota(jnp.int32, (rows, tk), 1)

    def step(j, masked):
        start = pl.multiple_of(j * tk, tk)
        k1 = ckv_ref[pl.ds(start, tk), :]
        k2 = kpe_ref[pl.ds(start, tk), :]
        s = _dot_nt(q1, k1) + _dot_nt(q2, k2)
        if masked:
            s = jnp.where(k_off + j * tk <= q_pos, s, NEG_INF)
        _softmax_update(s, k1, m_ref, l_ref, acc_ref)

    n_full = (c * tq) // tk

    def body(j, carry):
        step(j, False)
        return carry

    lax.fori_loop(0, n_full, body, 0)
    step(n_full, True)
    out = acc_ref[...] / l_ref[...]
    out_ref[...] = out.reshape(MLA_HEADS, tq, MLA_KV_LORA)


def _prompt_mla(qlat, qpe, ckvb, kpeb, olat_shape, batch, seq):
    tq = min(Q_BLOCK, seq)
    tk = min(512, seq)
    nq = seq // tq
    rows = MLA_HEADS * tq
    return pl.pallas_call(
        functools.partial(_prompt_mla_kernel, tq=tq, tk=tk),
        grid=(batch, nq),
        in_specs=[
            pl.BlockSpec((MLA_HEADS, tq, MLA_KV_LORA), lambda b, c: (0, b * nq + c, 0)),
            pl.BlockSpec((MLA_HEADS, tq, MLA_ROPE), lambda b, c: (0, b * nq + c, 0)),
            pl.BlockSpec((seq, MLA_KV_LORA), lambda b, c: (b, 0)),
            pl.BlockSpec((seq, MLA_ROPE), lambda b, c: (b, 0)),
        ],
        out_specs=pl.BlockSpec((MLA_HEADS, tq, MLA_KV_LORA), lambda b, c: (0, b * nq + c, 0)),
        out_shape=olat_shape,
        scratch_shapes=[pltpu.VMEM((rows, 1), F32), pltpu.VMEM((rows, 1), F32),
                        pltpu.VMEM((rows, MLA_KV_LORA), F32)],
        compiler_params=_params("parallel", "arbitrary"),
        name="prompt_mla",
    )(qlat, qpe, ckvb, kpeb)


def _prompt_moba_kernel(qm_ref, kb_ref, vb_ref, kmean_ref, bown_ref, bprev_ref, bfar_ref, out_ref,
                        m_ref, l_ref, acc_ref, *, tq, n_blk):
    c = pl.program_id(2)
    rows = MOBA_GROUP * tq
    own = (c * tq) // MOBA_BLOCK
    q = qm_ref[...].reshape(rows, MOBA_HEAD_DIM)
    qb = (q * MOBA_HEAD_DIM ** -0.5).astype(BF16)

    s_blk = _dot_nt(q, kmean_ref[0], precision=lax.Precision.HIGHEST)
    blk = lax.broadcasted_iota(jnp.int32, (rows, n_blk), 1)
    s_blk = jnp.where(blk < own, s_blk, NEG_INF)
    sel = jnp.where(blk < own, _top_mask(s_blk, MOBA_TOPK), 0.0)

    _softmax_init(m_ref, l_ref, acc_ref)

    def attend(j, bias):
        start = pl.multiple_of(j * MOBA_BLOCK, MOBA_BLOCK)
        k = kb_ref[0, pl.ds(start, MOBA_BLOCK), :]
        v = vb_ref[0, pl.ds(start, MOBA_BLOCK), :]
        picked = jnp.sum(jnp.where(blk == j, sel, 0.0), axis=-1, keepdims=True)
        s = jnp.where(picked > 0.5, _dot_nt(qb, k) + bias, NEG_INF)
        _softmax_update(s, v, m_ref, l_ref, acc_ref)

    def far(j, carry):
        attend(j, bfar_ref[0])
        return carry

    lax.fori_loop(0, own - 1, far, 0)

    @pl.when(own >= 1)
    def _():
        attend(own - 1, bprev_ref[0, 0])

    start = pl.multiple_of(own * MOBA_BLOCK, MOBA_BLOCK)
    k = kb_ref[0, pl.ds(start, MOBA_BLOCK), :]
    v = vb_ref[0, pl.ds(start, MOBA_BLOCK), :]
    _softmax_update(_dot_nt(qb, k) + bown_ref[0, 0], v, m_ref, l_ref, acc_ref)
    out = acc_ref[...] / l_ref[...]
    out_ref[...] = out.reshape(MOBA_GROUP, tq, MOBA_HEAD_DIM)


def _prompt_moba(qm, kb, vb, kmean_g, bias_own, bias_prev, bias_far, out_shape, batch, seq):
    tq = Q_BLOCK
    nq = seq // tq
    n_blk = seq // MOBA_BLOCK
    rows = MOBA_GROUP * tq
    per_block = MOBA_BLOCK // tq
    return pl.pallas_call(
        functools.partial(_prompt_moba_kernel, tq=tq, n_blk=n_blk),
        grid=(batch, MOBA_KV_HEADS, nq),
        in_specs=[
            pl.BlockSpec((MOBA_GROUP, tq, MOBA_HEAD_DIM), lambda b, g, c: (g, b * nq + c, 0)),
            pl.BlockSpec((1, seq, MOBA_HEAD_DIM), lambda b, g, c: (g, b, 0)),
            pl.BlockSpec((1, seq, MOBA_HEAD_DIM), lambda b, g, c: (g, b, 0)),
            pl.BlockSpec((1, n_blk, MOBA_HEAD_DIM), lambda b, g, c: (g, b, 0)),
            pl.BlockSpec((1, 1, rows, MOBA_BLOCK), lambda b, g, c: (g, c % per_block, 0, 0)),
            pl.BlockSpec((1, 1, rows, MOBA_BLOCK), lambda b, g, c: (g, c % per_block, 0, 0)),
            pl.BlockSpec((1, rows, 1), lambda b, g, c: (g, 0, 0)),
        ],
        out_specs=pl.BlockSpec((MOBA_GROUP, tq, MOBA_HEAD_DIM), lambda b, g, c: (g, b * nq + c, 0)),
        out_shape=out_shape,
        scratch_shapes=[pltpu.VMEM((rows, 1), F32), pltpu.VMEM((rows, 1), F32),
                        pltpu.VMEM((rows, MOBA_HEAD_DIM), F32)],
        compiler_params=_params("parallel", "parallel", "arbitrary"),
        name="prompt_moba",
    )(qm, kb, vb, kmean_g, bias_own, bias_prev, bias_far)


def _page_specs(block_shape, group):
    nd = len(block_shape)

    def make(k):
        def index_map(s, i, pt):
            return (0, pt[s, i * group + k]) + (0,) * (nd - 2)
        return pl.BlockSpec(block_shape, index_map)

    return [make(k) for k in range(group)]


def _block_mean_kernel(pt_ref, *refs, group):
    k_refs, out_ref = refs[:group], refs[group]
    inv = 1.0 / PAGE_SIZE
    ppb = MOBA_BLOCK // PAGE_SIZE
    for g in range(MOBA_KV_HEADS):
        rows = []
        for jb in range(group // ppb):
            acc = jnp.zeros((1, MOBA_HEAD_DIM), F32)
            for p in range(ppb):
                acc = acc + jnp.sum(k_refs[jb * ppb + p][0, 0, g], axis=0, keepdims=True) * inv
            rows.append(acc * (1.0 / ppb))
        out_ref[0, g] = jnp.concatenate(rows, axis=0)


def _block_means(page_table, cache_k, group):
    n_seq, n_pages = page_table.shape
    ppb = MOBA_BLOCK // PAGE_SIZE
    n_blk = n_pages // ppb
    page = (1, 1, MOBA_KV_HEADS, PAGE_SIZE, MOBA_HEAD_DIM)
    return pl.pallas_call(
        functools.partial(_block_mean_kernel, group=group),
        grid_spec=pltpu.PrefetchScalarGridSpec(
            num_scalar_prefetch=1,
            grid=(n_seq, n_pages // group),
            in_specs=_page_specs(page, group),
            out_specs=pl.BlockSpec((1, MOBA_KV_HEADS, group // ppb, MOBA_HEAD_DIM),
                                   lambda s, i, pt: (s, 0, i, 0)),
        ),
        out_shape=jax.ShapeDtypeStruct((n_seq, MOBA_KV_HEADS, n_blk, MOBA_HEAD_DIM), F32),
        compiler_params=_params("parallel", "arbitrary"),
        name="sample_block_means",
    )(page_table, *([cache_k] * group))


def _sample_moba_kernel(pt_ref, *refs, group, ds):
    k_refs, v_refs = refs[:group], refs[group:2 * group]
    (qm_ref, bm_ref, knew_ref, vnew_ref, expand_ref, blast_ref, bfar_ref, bnew_ref,
     out_ref, sel_ref, m_ref, l_ref, acc_ref) = refs[2 * group:]
    i = pl.program_id(1)
    last = pl.num_programs(1) - 1
    rows = MOBA_GROUP * ds
    scale = MOBA_HEAD_DIM ** -0.5

    @pl.when(i == 0)
    def _():
        _softmax_init(m_ref, l_ref, acc_ref)
        for g in range(MOBA_KV_HEADS):
            q = qm_ref[g * MOBA_GROUP:(g + 1) * MOBA_GROUP].reshape(rows, MOBA_HEAD_DIM)
            s_blk = _dot_nt(q, bm_ref[0, g], precision=lax.Precision.HIGHEST)
            sel_ref[g] = _top_mask(s_blk, MOBA_TOPK)

    for g in range(MOBA_KV_HEADS):
        q = qm_ref[g * MOBA_GROUP:(g + 1) * MOBA_GROUP].reshape(rows, MOBA_HEAD_DIM)
        qb = (q * scale).astype(BF16)
        k = jnp.concatenate([r[0, 0, g] for r in k_refs], axis=0).astype(BF16)
        v = jnp.concatenate([r[0, 0, g] for r in v_refs], axis=0).astype(BF16)
        picked = _dot(sel_ref[g].astype(BF16), expand_ref[...])
        bias = jnp.where(i == last, blast_ref[g], bfar_ref[g])
        s = jnp.where(picked > 0.5, _dot_nt(qb, k) + bias, NEG_INF)
        _softmax_update(s, v, m_ref.at[g], l_ref.at[g], acc_ref.at[g])

    @pl.when(i == last)
    def _():
        for g in range(MOBA_KV_HEADS):
            q = qm_ref[g * MOBA_GROUP:(g + 1) * MOBA_GROUP].reshape(rows, MOBA_HEAD_DIM)
            qb = (q * scale).astype(BF16)
            k = knew_ref[:, g * MOBA_HEAD_DIM:(g + 1) * MOBA_HEAD_DIM].astype(BF16)
            v = vnew_ref[:, g * MOBA_HEAD_DIM:(g + 1) * MOBA_HEAD_DIM].astype(BF16)
            _softmax_update(_dot_nt(qb, k) + bnew_ref[g], v, m_ref.at[g], l_ref.at[g], acc_ref.at[g])
            out = acc_ref[g] / l_ref[g]
            out_ref[g * MOBA_GROUP:(g + 1) * MOBA_GROUP] = out.reshape(MOBA_GROUP, ds, MOBA_HEAD_DIM)


def _sample_moba(page_table, cache_k, cache_v, qm, blk_mean, km, vm, expand, bias_last, bias_far, bias_new,
                 out_shape, n_prompt, ds, group):
    n_seq, n_pages = page_table.shape
    n_blk = blk_mean.shape[2]
    rows = MOBA_GROUP * ds
    keys = group * PAGE_SIZE
    page = (1, 1, MOBA_KV_HEADS, PAGE_SIZE, MOBA_HEAD_DIM)
    tok0 = n_prompt // ds
    kvw = MOBA_KV_HEADS * MOBA_HEAD_DIM
    const3 = lambda a: pl.BlockSpec(a.shape, lambda s, i, pt: (0, 0, 0))
    return pl.pallas_call(
        functools.partial(_sample_moba_kernel, group=group, ds=ds),
        grid_spec=pltpu.PrefetchScalarGridSpec(
            num_scalar_prefetch=1,
            grid=(n_seq, n_pages // group),
            in_specs=_page_specs(page, group) + _page_specs(page, group) + [
                pl.BlockSpec((MOBA_HEADS, ds, MOBA_HEAD_DIM), lambda s, i, pt: (0, tok0 + s, 0)),
                pl.BlockSpec((1, MOBA_KV_HEADS, n_blk, MOBA_HEAD_DIM), lambda s, i, pt: (s, 0, 0, 0)),
                pl.BlockSpec((ds, kvw), lambda s, i, pt: (tok0 + s, 0)),
                pl.BlockSpec((ds, kvw), lambda s, i, pt: (tok0 + s, 0)),
                pl.BlockSpec((n_blk, keys), lambda s, i, pt: (0, i)),
                const3(bias_last), const3(bias_far), const3(bias_new),
            ],
            out_specs=pl.BlockSpec((MOBA_HEADS, ds, MOBA_HEAD_DIM), lambda s, i, pt: (0, s, 0)),
            scratch_shapes=[pltpu.VMEM((MOBA_KV_HEADS, rows, n_blk), F32),
                            pltpu.VMEM((MOBA_KV_HEADS, rows, 1), F32),
                            pltpu.VMEM((MOBA_KV_HEADS, rows, 1), F32),
                            pltpu.VMEM((MOBA_KV_HEADS, rows, MOBA_HEAD_DIM), F32)],
        ),
        out_shape=out_shape,
        compiler_params=_params("parallel", "arbitrary"),
        name="sample_moba",
    )(page_table, *([cache_k] * group), *([cache_v] * group), qm, blk_mean, km, vm, expand,
      bias_last, bias_far, bias_new)


def _sample_mla_kernel(pt_ref, *refs, group, ds):
    c_refs, p_refs = refs[:group], refs[group:2 * group]
    qlat_ref, qpe_ref, cnew_ref, pnew_ref, out_ref, m_ref, l_ref, acc_ref = refs[2 * group:]
    i = pl.program_id(1)
    last = pl.num_programs(1) - 1
    rows = MLA_HEADS * ds
    q1 = qlat_ref[...].reshape(rows, MLA_KV_LORA).astype(BF16)
    q2 = qpe_ref[...].reshape(rows, MLA_ROPE).astype(BF16)

    @pl.when(i == 0)
    def _():
        _softmax_init(m_ref, l_ref, acc_ref)

    k1 = jnp.concatenate([r[0, 0] for r in c_refs], axis=0).astype(BF16)
    k2 = jnp.concatenate([r[0, 0] for r in p_refs], axis=0).astype(BF16)
    _softmax_update(_dot_nt(q1, k1) + _dot_nt(q2, k2), k1, m_ref, l_ref, acc_ref)

    @pl.when(i == last)
    def _():
        c_new = cnew_ref[...].astype(BF16)
        s = _dot_nt(q1, c_new) + _dot_nt(q2, pnew_ref[...].astype(BF16))
        tq = jnp.bitwise_and(lax.broadcasted_iota(jnp.int32, (rows, ds), 0), ds - 1)
        tk = lax.broadcasted_iota(jnp.int32, (rows, ds), 1)
        _softmax_update(jnp.where(tk <= tq, s, NEG_INF), c_new, m_ref, l_ref, acc_ref)
        out = acc_ref[...] / l_ref[...]
        out_ref[...] = out.reshape(MLA_HEADS, ds, MLA_KV_LORA)


def _sample_mla(page_table, cache_ckv, cache_kpe, qlat, qpe, ckv, kpe, out_shape, n_prompt, ds, group):
    n_seq, n_pages = page_table.shape
    rows = MLA_HEADS * ds
    tok0 = n_prompt // ds
    return pl.pallas_call(
        functools.partial(_sample_mla_kernel, group=group, ds=ds),
        grid_spec=pltpu.PrefetchScalarGridSpec(
            num_scalar_prefetch=1,
            grid=(n_seq, n_pages // group),
            in_specs=_page_specs((1, 1, PAGE_SIZE, MLA_KV_LORA), group)
            + _page_specs((1, 1, PAGE_SIZE, MLA_ROPE), group) + [
                pl.BlockSpec((MLA_HEADS, ds, MLA_KV_LORA), lambda s, i, pt: (0, tok0 + s, 0)),
                pl.BlockSpec((MLA_HEADS, ds, MLA_ROPE), lambda s, i, pt: (0, tok0 + s, 0)),
                pl.BlockSpec((ds, MLA_KV_LORA), lambda s, i, pt: (tok0 + s, 0)),
                pl.BlockSpec((ds, MLA_ROPE), lambda s, i, pt: (tok0 + s, 0)),
            ],
            out_specs=pl.BlockSpec((MLA_HEADS, ds, MLA_KV_LORA), lambda s, i, pt: (0, s, 0)),
            scratch_shapes=[pltpu.VMEM((rows, 1), F32), pltpu.VMEM((rows, 1), F32),
                            pltpu.VMEM((rows, MLA_KV_LORA), F32)],
        ),
        out_shape=out_shape,
        compiler_params=_params("parallel", "arbitrary"),
        name="sample_mla",
    )(page_table, *([cache_ckv] * group), *([cache_kpe] * group), qlat, qpe, ckv, kpe)


def _merge_kernel(omoba_ref, olat_ref, gates_ref, x_ref, wuv_ref, wbm_ref, wbl_ref, wout_ref, g_ref, b_ref,
                  h_ref, hb_ref, *, alpha):
    y_moba = None
    y_mla = None
    for h in range(MOBA_HEADS):
        t = _dot(omoba_ref[h].astype(BF16), wbm_ref[h])
        y_moba = t if y_moba is None else y_moba + t
    for h in range(MLA_HEADS):
        o = _dot(olat_ref[h].astype(BF16), wuv_ref[h]).astype(BF16)
        t = _dot(o, wbl_ref[h])
        y_mla = t if y_mla is None else y_mla + t
    merged = gates_ref[:, :D_MODEL] * y_moba + gates_ref[:, D_MODEL:] * y_mla
    mix = _dot(merged.astype(BF16), wout_ref[...])
    h = _layer_norm(alpha * x_ref[...] + mix, g_ref[...], b_ref[...])
    h_ref[...] = h
    hb_ref[...] = h.astype(BF16)


def _merge(omoba, olat, gates, x_all, wuv, wbm, wbl, wout, ln_g, ln_b, alpha, tb):
    n = x_all.shape[0]
    row = lambda w: pl.BlockSpec((tb, w), lambda i: (i, 0))
    full = lambda a: pl.BlockSpec(a.shape, lambda i: (0,) * a.ndim)
    return pl.pallas_call(
        functools.partial(_merge_kernel, alpha=alpha),
        grid=(n // tb,),
        in_specs=[pl.BlockSpec((MOBA_HEADS, tb, MOBA_HEAD_DIM), lambda i: (0, i, 0)),
                  pl.BlockSpec((MLA_HEADS, tb, MLA_KV_LORA), lambda i: (0, i, 0)),
                  row(2 * D_MODEL), row(D_MODEL), full(wuv), full(wbm), full(wbl), full(wout),
                  full(ln_g), full(ln_b)],
        out_specs=(row(D_MODEL), row(D_MODEL)),
        out_shape=(jax.ShapeDtypeStruct((n, D_MODEL), F32), jax.ShapeDtypeStruct((n, D_MODEL), BF16)),
        compiler_params=_params("parallel"),
        name="merge_ln1",
    )(omoba, olat, gates, x_all, wuv, wbm, wbl, wout, ln_g, ln_b)


def _peer_pairs():
    return [(a, b) for a in range(PEER_TOPK) for b in range(PEER_TOPK) if (a + 1) * (b + 1) <= PEER_TOPK]


def _top_values(s, n):
    vals = []
    for _ in range(n):
        m = jnp.max(s, axis=0, keepdims=True)
        vals.append(m)
        s = jnp.where(s == m, KNOCKED_OUT, s)
    return vals


def _peer_select_kernel(hb_ref, wq_ref, k1_ref, k2_ref, a_ref, c_ref, b_ref, s2_ref):
    q = _dot(hb_ref[...], wq_ref[...])
    half = PEER_KEY_DIM // 2
    pairs = _peer_pairs()
    for h in range(PEER_HEADS):
        q1 = q[:, h * PEER_KEY_DIM:h * PEER_KEY_DIM + half]
        q2 = q[:, h * PEER_KEY_DIM + half:(h + 1) * PEER_KEY_DIM]
        s1 = _dot_nt(k1_ref[h], q1, precision=lax.Precision.HIGHEST)
        s2 = _dot_nt(k2_ref[h], q2, precision=lax.Precision.HIGHEST)
        v1 = _top_values(s1, PEER_TOPK)
        v2 = _top_values(s2, PEER_TOPK)
        cand = jnp.concatenate([v1[a] + v2[b] for a, b in pairs], axis=0)
        thr = _top_values(cand, PEER_TOPK)[-1]
        top = v1[0] + v2[0]
        z = jnp.sum(jnp.where(cand >= thr, jnp.exp(cand - top), 0.0), axis=0, keepdims=True)
        cut = jnp.full(s1.shape, -KNOCKED_OUT, F32)
        for a in range(PEER_TOPK):
            lim = None
            for b in range(PEER_TOPK // (a + 1)):
                t = jnp.where(v1[a] + v2[b] >= thr, v2[b], -KNOCKED_OUT)
                lim = t if lim is None else jnp.minimum(lim, t)
            cut = jnp.where(s1 == v1[a], lim, cut)
        a_ref[h] = jnp.exp(s1 - v1[0]) / z
        c_ref[h] = cut
        b_ref[h] = jnp.exp(s2 - v2[0])
        s2_ref[h] = s2


def _peer_select(hb, wq, k1, k2, tb):
    n = hb.shape[0]
    full = lambda a: pl.BlockSpec(a.shape, lambda i: (0,) * a.ndim)
    sel = pl.BlockSpec((PEER_HEADS, PEER_NKEYS, tb), lambda i: (0, 0, i))
    shape = jax.ShapeDtypeStruct((PEER_HEADS, PEER_NKEYS, n), F32)
    return pl.pallas_call(
        _peer_select_kernel,
        grid=(n // tb,),
        in_specs=[pl.BlockSpec((tb, D_MODEL), lambda i: (i, 0)), full(wq), full(k1), full(k2)],
        out_specs=(sel, sel, sel, sel),
        out_shape=(shape, shape, shape, shape),
        compiler_params=_params("parallel"),
        name="peer_select",
    )(hb, wq, k1, k2)


def _peer_expert_kernel(hb_ref, h_ref, u_ref, vt_ref, a_ref, c_ref, b_ref, s2_ref, g_ref, bias_ref,
                        out_ref, acc_ref, *, alpha, rows_per_step):
    e = pl.program_id(1)

    @pl.when(e == 0)
    def _():
        acc_ref[...] = jnp.zeros(acc_ref.shape, F32)

    act = _dot_nt(u_ref[...], hb_ref[...])
    act = 0.5 * act * (1.0 + lax.erf(act * (2.0 ** -0.5)))
    parts = []
    for r in range(rows_per_step):
        i1 = e * rows_per_step + r
        w = None
        for h in range(PEER_HEADS):
            a_row = a_ref[h, pl.ds(i1, 1), :]
            c_row = c_ref[h, pl.ds(i1, 1), :]
            t = jnp.where(s2_ref[h] >= c_row, a_row * b_ref[h], 0.0)
            w = t if w is None else w + t
        parts.append((w * act[r * PEER_NKEYS:(r + 1) * PEER_NKEYS]).astype(BF16))
    gw = parts[0] if rows_per_step == 1 else jnp.concatenate(parts, axis=0)
    acc_ref[...] += _dot(vt_ref[...], gw)

    @pl.when(e == pl.num_programs(1) - 1)
    def _():
        f = acc_ref[...].T
        out_ref[...] = _layer_norm(alpha * h_ref[...] + f, g_ref[...], bias_ref[...])


def _peer_expert(hb, h, u_b, vt_b, sel_a, sel_c, sel_b, sel_s2, ln_g, ln_b, alpha, tb, rows_per_step):
    n = hb.shape[0]
    n_exp = u_b.shape[0]
    et = rows_per_step * PEER_NKEYS
    sel = pl.BlockSpec((PEER_HEADS, PEER_NKEYS, tb), lambda i, e: (0, 0, i))
    vec = pl.BlockSpec((1, D_MODEL), lambda i, e: (0, 0))
    return pl.pallas_call(
        functools.partial(_peer_expert_kernel, alpha=alpha, rows_per_step=rows_per_step),
        grid=(n // tb, n_exp // et),
        in_specs=[pl.BlockSpec((tb, D_MODEL), lambda i, e: (i, 0)),
                  pl.BlockSpec((tb, D_MODEL), lambda i, e: (i, 0)),
                  pl.BlockSpec((et, D_MODEL), lambda i, e: (e, 0)),
                  pl.BlockSpec((D_MODEL, et), lambda i, e: (0, e)),
                  sel, sel, sel, sel, vec, vec],
        out_specs=pl.BlockSpec((tb, D_MODEL), lambda i, e: (i, 0)),
        out_shape=jax.ShapeDtypeStruct((n, D_MODEL), F32),
        scratch_shapes=[pltpu.VMEM((D_MODEL, tb), F32)],
        compiler_params=_params("parallel", "arbitrary"),
        name="peer_experts",
    )(hb, h, u_b, vt_b, sel_a, sel_c, sel_b, sel_s2, ln_g, ln_b)


def _rel_bucket_static(dist):
    n = np.maximum(dist, 0)
    max_exact = REL_BUCKETS // 2
    nf = np.maximum(n, 1).astype(np.float32)
    large = max_exact + (np.log(nf / max_exact) / math.log(REL_MAX_DIST / max_exact)
                         * (REL_BUCKETS - max_exact)).astype(np.int32)
    return np.where(n < max_exact, n, np.minimum(large, REL_BUCKETS - 1)).astype(np.int32)


def _bias_tile(rel_bias, dist, causal):
    bucket = _rel_bucket_static(dist)
    tile = jnp.transpose(rel_bias.astype(F32)[bucket], (2, 0, 1))
    if causal:
        tile = jnp.where(jnp.asarray(dist >= 0)[None], tile, NEG_INF)
    return tile.reshape(MOBA_KV_HEADS, MOBA_GROUP * dist.shape[0], dist.shape[1])


def _rope_table(pos):
    half = MLA_ROPE // 2
    inv_freq = ROPE_THETA ** (-jnp.arange(half, dtype=F32) / half)
    ang = pos.astype(F32)[:, None] * inv_freq[None, :]
    cos, sin = jnp.cos(ang), jnp.sin(ang)
    cos_full = jnp.tile(jnp.concatenate([cos, cos], -1), (1, MLA_HEADS))
    sin_signed = jnp.tile(jnp.concatenate([-sin, sin], -1), (1, MLA_HEADS))
    return jnp.concatenate([cos_full, sin_signed], -1)


def _swap_halves(w):
    half = w.shape[-1] // 2
    return jnp.concatenate([w[..., half:], w[..., :half]], -1)


def _largest_divisor(n, cap):
    d = min(n, cap)
    while n % d:
        d -= 1
    return d


def kernel(x_prompt, x_sample, cache_moba_k, cache_moba_v, cache_mla_ckv, cache_mla_kpe, page_table, w_in, b_gate, mla_q_norm, mla_w_uq, mla_kv_norm, mla_w_uk, mla_w_uv, rel_bias, w_br_moba, w_br_mla, w_out, ln1_g, ln1_b, peer_wq, peer_k1, peer_k2, peer_u, peer_v, ln2_g, ln2_b):
    depth = w_in.shape[0]
    assert depth == 1, "single-layer step"
    batch, seq, _ = x_prompt.shape
    n_seq, ds, _ = x_sample.shape
    n_pages = page_table.shape[1]
    past = n_pages * PAGE_SIZE
    n_prompt = batch * seq
    n_sample = n_seq * ds
    n_all = n_prompt + n_sample
    assert seq % MOBA_BLOCK == 0 and seq // MOBA_BLOCK >= MOBA_TOPK
    assert past % MOBA_BLOCK == 0 and past // MOBA_BLOCK >= MOBA_TOPK
    assert n_prompt % MOBA_BLOCK == 0 and n_sample % MOBA_BLOCK == 0 and ds == 8
    alpha = (2 * depth) ** 0.25
    layer = 0

    moba_w = MOBA_HEADS * MOBA_HEAD_DIM
    kv_w = MOBA_KV_HEADS * MOBA_HEAD_DIM
    w = w_in[layer]
    c0 = moba_w + 2 * kv_w
    wqkv = w[:, :c0].astype(BF16)
    lat_w = MLA_Q_LORA + MLA_KV_LORA
    kr_w = w[:, c0 + lat_w:c0 + lat_w + MLA_ROPE]
    pad = jnp.zeros((D_MODEL, 128 - 2 * MLA_ROPE), w.dtype)
    wlat = jnp.concatenate([w[:, c0:c0 + lat_w], kr_w, _swap_halves(kr_w), pad], -1).astype(BF16)
    wgate = w[:, c0 + lat_w + MLA_ROPE:].astype(BF16)
    bgate = b_gate[layer].reshape(1, -1)
    uq = mla_w_uq[layer]
    uq_pe = uq[..., MLA_NOPE:]
    wuq = jnp.concatenate([uq[..., :MLA_NOPE].reshape(MLA_Q_LORA, -1), uq_pe.reshape(MLA_Q_LORA, -1),
                           _swap_halves(uq_pe).reshape(MLA_Q_LORA, -1)], -1).astype(BF16)
    wuk = jnp.transpose(mla_w_uk[layer], (1, 2, 0)).astype(BF16)
    wuv = jnp.transpose(mla_w_uv[layer], (1, 0, 2)).astype(BF16)
    wbm = w_br_moba[layer].reshape(MOBA_HEADS, MOBA_HEAD_DIM, D_MODEL).astype(BF16)
    wbl = w_br_mla[layer].reshape(MLA_HEADS, MLA_V, D_MODEL).astype(BF16)
    wout = w_out[layer].astype(BF16)
    wq = peer_wq[layer].astype(BF16)
    u_b = peer_u[layer].astype(BF16)
    vt_b = peer_v[layer].astype(BF16).T
    vec = lambda a: a[layer].reshape(1, -1)

    pos_all = jnp.concatenate([jnp.tile(jnp.arange(seq), batch), jnp.tile(past + jnp.arange(ds), n_seq)])
    rope_tab = _rope_table(pos_all)
    qi = np.arange(Q_BLOCK)[:, None]
    ki = np.arange(MOBA_BLOCK)[None, :]
    per_block = MOBA_BLOCK // Q_BLOCK
    bias_own = jnp.stack([_bias_tile(rel_bias, o * Q_BLOCK + qi - ki, True) for o in range(per_block)], 1)
    bias_prev = jnp.stack([_bias_tile(rel_bias, MOBA_BLOCK + o * Q_BLOCK + qi - ki, False)
                           for o in range(per_block)], 1)
    far_row = jnp.repeat(rel_bias[REL_BUCKETS - 1].astype(F32), Q_BLOCK).reshape(MOBA_KV_HEADS, -1, 1)
    ti = np.arange(ds)[:, None]
    group = _largest_divisor(n_pages, 16)
    group -= group % (MOBA_BLOCK // PAGE_SIZE)
    keys = group * PAGE_SIZE
    far_s = jnp.repeat(rel_bias[REL_BUCKETS - 1].astype(F32), ds).reshape(MOBA_KV_HEADS, -1, 1)
    tail = _bias_tile(rel_bias, past + ti - (past - MOBA_BLOCK + ki), False)
    bias_last = jnp.concatenate([jnp.broadcast_to(far_s, far_s.shape[:2] + (keys - MOBA_BLOCK,)), tail], -1)
    bias_new = _bias_tile(rel_bias, ti - np.arange(ds)[None, :], True)
    n_blk_s = past // MOBA_BLOCK
    expand = jnp.asarray(np.repeat(np.eye(n_blk_s, dtype=np.float32), MOBA_BLOCK, axis=1), BF16)

    x_all = jnp.concatenate([x_prompt.reshape(n_prompt, D_MODEL), x_sample.reshape(n_sample, D_MODEL)], 0)
    (qm, km, vm, kb, vb, kmean, ckv, ckvb, kpe, kpeb, gates, qlat, qpe) = _project(
        x_all, rope_tab, wqkv, wlat, wgate, bgate, vec(mla_q_norm), vec(mla_kv_norm), wuq, wuk)
    kmean_g = jnp.transpose(kmean.reshape(-1, MOBA_KV_HEADS, MOBA_HEAD_DIM), (1, 0, 2))

    olat_shape = jax.ShapeDtypeStruct((MLA_HEADS, n_prompt, MLA_KV_LORA), F32)
    olat_p = _prompt_mla(qlat, qpe, ckvb, kpeb, olat_shape, batch, seq)
    omoba_shape = jax.ShapeDtypeStruct((MOBA_HEADS, n_prompt, MOBA_HEAD_DIM), F32)
    omoba_p = _prompt_moba(qm, kb, vb, kmean_g, bias_own, bias_prev, far_row, omoba_shape, batch, seq)

    blk_mean = _block_means(page_table, cache_moba_k, group)
    omoba_s = _sample_moba(page_table, cache_moba_k, cache_moba_v, qm, blk_mean, km, vm, expand, bias_last,
                           far_s, bias_new,
                           jax.ShapeDtypeStruct((MOBA_HEADS, n_sample, MOBA_HEAD_DIM), F32),
                           n_prompt, ds, group)
    olat_s = _sample_mla(page_table, cache_mla_ckv, cache_mla_kpe, qlat, qpe, ckv, kpe,
                         jax.ShapeDtypeStruct((MLA_HEADS, n_sample, MLA_KV_LORA), F32),
                         n_prompt, ds, group)
    omoba = jnp.concatenate([omoba_p, omoba_s], 1)
    olat = jnp.concatenate([olat_p, olat_s], 1)

    tb = _largest_divisor(n_all, 256)
    h, hb = _merge(omoba, olat, gates, x_all, wuv, wbm, wbl, wout, vec(ln1_g), vec(ln1_b), alpha, tb)
    sel_a, sel_c, sel_b, sel_s2 = _peer_select(hb, wq, peer_k1[layer], peer_k2[layer], tb)
    tb_e = _largest_divisor(n_all, 512)
    y = _peer_expert(hb, h, u_b, vt_b, sel_a, sel_c, sel_b, sel_s2, vec(ln2_g), vec(ln2_b), alpha, tb_e, 2)

    def kv_out(a, n0, n1, lead):
        return a[n0:n1].reshape((depth,) + lead + (MOBA_KV_HEADS, MOBA_HEAD_DIM))

    lead_p, lead_s = (batch, seq), (n_seq, ds)
    return (
        y[:n_prompt].reshape(batch, seq, D_MODEL),
        y[n_prompt:].reshape(n_seq, ds, D_MODEL),
        kv_out(km, 0, n_prompt, lead_p), kv_out(vm, 0, n_prompt, lead_p),
        ckv[:n_prompt].reshape((depth,) + lead_p + (MLA_KV_LORA,)),
        kpe[:n_prompt].reshape((depth,) + lead_p + (MLA_ROPE,)),
        kv_out(km, n_prompt, n_all, lead_s), kv_out(vm, n_prompt, n_all, lead_s),
        ckv[n_prompt:].reshape((depth,) + lead_s + (MLA_KV_LORA,)),
        kpe[n_prompt:].reshape((depth,) + lead_s + (MLA_ROPE,)),
    )
```

```python
import functools
import math

import numpy as np
import jax
import jax.numpy as jnp
from jax import lax
from jax.experimental import pallas as pl
from jax.experimental.pallas import tpu as pltpu

D_MODEL = 1024
PAGE_SIZE = 128
MOBA_HEADS = 8
MOBA_KV_HEADS = 4
MOBA_GROUP = MOBA_HEADS // MOBA_KV_HEADS
MOBA_HEAD_DIM = 64
MOBA_BLOCK = 256
MOBA_TOPK = 3
BLOCKS_PER_STEP = 4
MEAN_PAGES_PER_STEP = 16
ATT_PAGES_PER_STEP = 8
SEQS_PER_STEP = 2
MLA_HEADS = 8
MLA_NOPE = 64
MLA_ROPE = 32
MLA_V = 64
MLA_Q_LORA = 256
MLA_KV_LORA = 256
ROPE_THETA = 10000.0
REL_BUCKETS = 32
REL_MAX_DIST = 128
PEER_HEADS = 8
PEER_NKEYS = 128
PEER_KEY_DIM = 128
PEER_TOPK = 16
PEER_ROWS_PER_STEP = 4
PEER_SUB_ROWS = 2
Q_BLOCK = 128
LN_EPS = 1e-5
RMS_EPS = 1e-6
NEG_INF = -1e30
KNOCKED_OUT = -3e38

KV_W = MOBA_KV_HEADS * MOBA_HEAD_DIM
MOBA_W = MOBA_HEADS * MOBA_HEAD_DIM
NOPE_W = MLA_HEADS * MLA_NOPE
PE_W = MLA_HEADS * MLA_ROPE
MLA_SCALE = (MLA_NOPE + MLA_ROPE) ** -0.5
MOBA_SCALE = MOBA_HEAD_DIM ** -0.5

VMEM_LIMIT_BYTES = 56 * 1024 * 1024
SUBLANES, LANES = 8, 128

BF16 = jnp.bfloat16
F32 = jnp.float32
HIGHEST = lax.Precision.HIGHEST


def _dot(a, b, precision=None):
    return jnp.dot(a, b, precision=precision, preferred_element_type=F32)


def _dot_nt(a, b, precision=None):
    return lax.dot_general(a, b, (((1,), (1,)), ((), ())), precision=precision,
                           preferred_element_type=F32)


def _params(*sem):
    return pltpu.CompilerParams(dimension_semantics=sem, vmem_limit_bytes=VMEM_LIMIT_BYTES)


def _rms(x, g):
    return x * lax.rsqrt(jnp.mean(x * x, axis=-1, keepdims=True) + RMS_EPS) * g


def _layer_norm(x, g, b):
    mu = jnp.mean(x, axis=-1, keepdims=True)
    xc = x - mu
    var = jnp.mean(xc * xc, axis=-1, keepdims=True)
    return xc * lax.rsqrt(var + LN_EPS) * g + b


def _sigmoid(x):
    return 1.0 / (1.0 + jnp.exp(-x))


def _softmax_init(m_ref, l_ref, acc_ref):
    m_ref[...] = jnp.full(m_ref.shape, NEG_INF, F32)
    l_ref[...] = jnp.zeros(l_ref.shape, F32)
    acc_ref[...] = jnp.zeros(acc_ref.shape, F32)


def _softmax_update(s, pv, m_ref, l_ref, acc_ref):
    m_old = m_ref[...]
    m_new = jnp.maximum(m_old, jnp.max(s, axis=-1, keepdims=True))
    alpha = jnp.exp(m_old - m_new)
    p = jnp.exp(s - m_new)
    l_ref[...] = alpha * l_ref[...] + jnp.sum(p, axis=-1, keepdims=True)
    acc_ref[...] = alpha * acc_ref[...] + pv(p.astype(BF16))
    m_ref[...] = m_new


def _top_mask(s, n_sel, axis):
    pos = lax.broadcasted_iota(jnp.int32, s.shape, axis)
    sel = jnp.zeros(s.shape, F32)
    for _ in range(n_sel):
        m = jnp.max(s, axis=axis, keepdims=True)
        idx = jnp.min(jnp.where(s == m, pos, s.shape[axis]), axis=axis, keepdims=True)
        pick = pos == idx
        sel = jnp.where(pick, 1.0, sel)
        s = jnp.where(pick, KNOCKED_OUT, s)
    return sel


def _proj_common(x_ref, tab_ref, wkv_ref, wlat_ref, wgate_ref, bgate_ref, qn_ref, kvn_ref,
                 km_ref, vm_ref, ckv_ref, kpe_ref, gates_ref):
    xb = x_ref[...].astype(BF16)
    kv = _dot(xb, wkv_ref[...])
    km, vm = kv[:, :KV_W], kv[:, KV_W:]
    km_ref[...] = km
    vm_ref[...] = vm
    lat = _dot(xb, wlat_ref[...])
    cq = lat[:, :MLA_Q_LORA]
    ckv = lat[:, MLA_Q_LORA:MLA_Q_LORA + MLA_KV_LORA]
    base = MLA_Q_LORA + MLA_KV_LORA
    kr = lat[:, base:base + MLA_ROPE]
    kr_swapped = lat[:, base + MLA_ROPE:base + 2 * MLA_ROPE]
    ckv_n = _rms(ckv, kvn_ref[...])
    ckv_ref[...] = ckv_n
    kpe = kr * tab_ref[:, :MLA_ROPE] + kr_swapped * tab_ref[:, PE_W:PE_W + MLA_ROPE]
    kpe_ref[...] = kpe
    gates_ref[...] = _sigmoid(_dot_nt(wgate_ref[...], xb) + bgate_ref[...])
    cqn = _rms(cq, qn_ref[...]).astype(BF16)
    return xb, km, ckv_n, kpe, cqn


def _proj_prompt_kernel(x_ref, tab_ref, tabt_ref, wkv_ref, wlat_ref, wgate_ref, bgate_ref, qn_ref, kvn_ref,
                        wqt_ref, wvt_ref, wuqt_ref, wukt_ref,
                        km_ref, vm_ref, ckv_ref, kpe_ref, gates_ref,
                        qmt_ref, kb_ref, vbt_ref, kmean_ref, ckvb_ref, kpeb_ref, ckvt_ref, qlatt_ref, qpet_ref):
    xb, km, ckv_n, kpe, cqn = _proj_common(x_ref, tab_ref, wkv_ref, wlat_ref, wgate_ref, bgate_ref, qn_ref,
                                           kvn_ref, km_ref, vm_ref, ckv_ref, kpe_ref, gates_ref)
    q_t = _dot_nt(wqt_ref[...], xb)
    for h in range(MOBA_HEADS):
        qmt_ref[h] = q_t[h * MOBA_HEAD_DIM:(h + 1) * MOBA_HEAD_DIM]
    v_t = _dot_nt(wvt_ref[...], xb)
    for g in range(MOBA_KV_HEADS):
        kb_ref[g] = km[:, g * MOBA_HEAD_DIM:(g + 1) * MOBA_HEAD_DIM].astype(BF16)
        vbt_ref[g, 0] = v_t[g * MOBA_HEAD_DIM:(g + 1) * MOBA_HEAD_DIM].astype(BF16)
    kmean_ref[0] = jnp.mean(km, axis=0, keepdims=True)
    ckvb_ref[...] = ckv_n.astype(BF16)
    kpeb_ref[...] = kpe.astype(BF16)
    ckvt_ref[0] = ckv_n.T.astype(BF16)
    qq_t = _dot_nt(wuqt_ref[...], cqn)
    qpe_t = (qq_t[NOPE_W:NOPE_W + PE_W] * tabt_ref[:PE_W] + qq_t[NOPE_W + PE_W:] * tabt_ref[PE_W:]) * MLA_SCALE
    for h in range(MLA_HEADS):
        qpet_ref[h] = qpe_t[h * MLA_ROPE:(h + 1) * MLA_ROPE].astype(BF16)
        q_nope_t = qq_t[h * MLA_NOPE:(h + 1) * MLA_NOPE].astype(BF16)
        qlatt_ref[h] = (_dot(wukt_ref[h], q_nope_t) * MLA_SCALE).astype(BF16)


def _proj_sample_kernel(x_ref, tab_ref, wkv_ref, wlat_ref, wgate_ref, bgate_ref, qn_ref, kvn_ref,
                        wq_ref, wuq_ref, wuk_ref,
                        km_ref, vm_ref, ckv_ref, kpe_ref, gates_ref, qm_ref, qlat_ref, qpe_ref):
    xb, _, _, _, cqn = _proj_common(x_ref, tab_ref, wkv_ref, wlat_ref, wgate_ref, bgate_ref, qn_ref,
                                    kvn_ref, km_ref, vm_ref, ckv_ref, kpe_ref, gates_ref)
    q = _dot(xb, wq_ref[...])
    for h in range(MOBA_HEADS):
        qm_ref[h] = q[:, h * MOBA_HEAD_DIM:(h + 1) * MOBA_HEAD_DIM]
    qq = _dot(cqn, wuq_ref[...])
    qpe = (qq[:, NOPE_W:NOPE_W + PE_W] * tab_ref[:, :PE_W] + qq[:, NOPE_W + PE_W:] * tab_ref[:, PE_W:]) * MLA_SCALE
    for h in range(MLA_HEADS):
        qpe_ref[h] = qpe[:, h * MLA_ROPE:(h + 1) * MLA_ROPE]
        q_nope = qq[:, h * MLA_NOPE:(h + 1) * MLA_NOPE].astype(BF16)
        qlat_ref[h] = _dot(q_nope, wuk_ref[h]) * MLA_SCALE


def _project(x, tab, weights, extra_in, extra_shapes, extra_specs, body, name):
    n = x.shape[0]
    tb = MOBA_BLOCK
    row = lambda w: pl.BlockSpec((tb, w), lambda i: (i, 0))
    full = lambda a: pl.BlockSpec(a.shape, lambda i: (0,) * a.ndim)
    common_shapes = (
        jax.ShapeDtypeStruct((n, KV_W), F32), jax.ShapeDtypeStruct((n, KV_W), F32),
        jax.ShapeDtypeStruct((n, MLA_KV_LORA), F32), jax.ShapeDtypeStruct((n, MLA_ROPE), F32),
        jax.ShapeDtypeStruct((2 * D_MODEL, n), F32),
    )
    common_specs = (row(KV_W), row(KV_W), row(MLA_KV_LORA), row(MLA_ROPE),
                    pl.BlockSpec((2 * D_MODEL, tb), lambda i: (0, i)))
    operands = (x,) + tuple(tab) + tuple(weights) + tuple(extra_in)
    in_specs = [row(D_MODEL), row(tab[0].shape[1])]
    if len(tab) == 2:
        in_specs.append(pl.BlockSpec((tab[1].shape[0], tb), lambda i: (0, i)))
    in_specs += [full(a) for a in tuple(weights) + tuple(extra_in)]
    return pl.pallas_call(
        body,
        grid=(n // tb,),
        in_specs=in_specs,
        out_specs=common_specs + tuple(extra_specs),
        out_shape=common_shapes + tuple(extra_shapes),
        compiler_params=_params("parallel"),
        name=name,
    )(*operands)


def _project_prompt(x, tab, tab_t, weights, wq_t, wv_t, wuq_t, wuk_t):
    n = x.shape[0]
    tb = MOBA_BLOCK
    nb = n // tb
    shapes = (
        jax.ShapeDtypeStruct((MOBA_HEADS, MOBA_HEAD_DIM, n), F32),
        jax.ShapeDtypeStruct((MOBA_KV_HEADS, n, MOBA_HEAD_DIM), BF16),
        jax.ShapeDtypeStruct((MOBA_KV_HEADS, nb, MOBA_HEAD_DIM, tb), BF16),
        jax.ShapeDtypeStruct((nb, 1, KV_W), F32),
        jax.ShapeDtypeStruct((n, MLA_KV_LORA), BF16),
        jax.ShapeDtypeStruct((n, MLA_ROPE), BF16),
        jax.ShapeDtypeStruct((nb, MLA_KV_LORA, tb), BF16),
        jax.ShapeDtypeStruct((MLA_HEADS, MLA_KV_LORA, n), BF16),
        jax.ShapeDtypeStruct((MLA_HEADS, MLA_ROPE, n), BF16),
    )
    specs = (
        pl.BlockSpec((MOBA_HEADS, MOBA_HEAD_DIM, tb), lambda i: (0, 0, i)),
        pl.BlockSpec((MOBA_KV_HEADS, tb, MOBA_HEAD_DIM), lambda i: (0, i, 0)),
        pl.BlockSpec((MOBA_KV_HEADS, 1, MOBA_HEAD_DIM, tb), lambda i: (0, i, 0, 0)),
        pl.BlockSpec((1, 1, KV_W), lambda i: (i, 0, 0)),
        pl.BlockSpec((tb, MLA_KV_LORA), lambda i: (i, 0)),
        pl.BlockSpec((tb, MLA_ROPE), lambda i: (i, 0)),
        pl.BlockSpec((1, MLA_KV_LORA, tb), lambda i: (i, 0, 0)),
        pl.BlockSpec((MLA_HEADS, MLA_KV_LORA, tb), lambda i: (0, 0, i)),
        pl.BlockSpec((MLA_HEADS, MLA_ROPE, tb), lambda i: (0, 0, i)),
    )
    return _project(x, (tab, tab_t), weights, (wq_t, wv_t, wuq_t, wuk_t), shapes, specs,
                    _proj_prompt_kernel, "prompt_projection")


def _project_sample(x, tab, weights, wq, wuq, wuk):
    n = x.shape[0]
    tb = MOBA_BLOCK
    shapes = (
        jax.ShapeDtypeStruct((MOBA_HEADS, n, MOBA_HEAD_DIM), F32),
        jax.ShapeDtypeStruct((MLA_HEADS, n, MLA_KV_LORA), F32),
        jax.ShapeDtypeStruct((MLA_HEADS, n, MLA_ROPE), F32),
    )
    head = lambda nh, w: pl.BlockSpec((nh, tb, w), lambda i: (0, i, 0))
    specs = (head(MOBA_HEADS, MOBA_HEAD_DIM), head(MLA_HEADS, MLA_KV_LORA), head(MLA_HEADS, MLA_ROPE))
    return _project(x, (tab,), weights, (wq, wuq, wuk), shapes, specs, _proj_sample_kernel, "sample_projection")


def _prompt_mla_kernel(qlat_ref, qpe_ref, ckv_ref, kpe_ref, ckvt_ref, out_ref, wq_ref, wpe_ref,
                       m_ref, l_ref, acc_ref, *, tq, tk):
    c = pl.program_id(1)
    pair_cols = 2 * tq
    n_pairs = MLA_HEADS // 2
    n_sub = tk // MOBA_BLOCK
    for p in range(n_pairs):
        wq_ref[p] = jnp.concatenate([qlat_ref[2 * p], qlat_ref[2 * p + 1]], axis=1)
        wpe_ref[p] = jnp.concatenate([qpe_ref[2 * p], qpe_ref[2 * p + 1]], axis=1)
    _softmax_init(m_ref, l_ref, acc_ref)
    q_pos = c * tq + jnp.bitwise_and(lax.broadcasted_iota(jnp.int32, (tk, pair_cols), 1), tq - 1)
    k_off = lax.broadcasted_iota(jnp.int32, (tk, pair_cols), 0)

    def step(j, masked):
        start = pl.multiple_of(j * tk, tk)
        k1 = ckv_ref[pl.ds(start, tk), :]
        k2 = kpe_ref[pl.ds(start, tk), :]
        v_t = [ckvt_ref[j * n_sub + u] for u in range(n_sub)]
        scores = []
        for pr in range(n_pairs):
            s = _dot(k1, wq_ref[pr]) + _dot(k2, wpe_ref[pr])
            if masked:
                s = jnp.where(k_off + j * tk <= q_pos, s, NEG_INF)
            scores.append(s)
        m_old = m_ref[...]
        l_old = l_ref[...]
        m_new = jnp.concatenate(
            [jnp.maximum(m_old[:, pr * pair_cols:(pr + 1) * pair_cols], jnp.max(scores[pr], axis=0, keepdims=True))
             for pr in range(n_pairs)], axis=1)
        alpha = jnp.exp(m_old - m_new)
        probs = [jnp.exp(scores[pr] - m_new[:, pr * pair_cols:(pr + 1) * pair_cols]) for pr in range(n_pairs)]
        l_ref[...] = alpha * l_old + jnp.concatenate([jnp.sum(p, axis=0, keepdims=True) for p in probs], axis=1)
        m_ref[...] = m_new
        for pr in range(n_pairs):
            cols = slice(pr * pair_cols, (pr + 1) * pair_cols)
            p = probs[pr].astype(BF16)
            pv = None
            for u in range(n_sub):
                t = _dot(v_t[u], p[u * MOBA_BLOCK:(u + 1) * MOBA_BLOCK])
                pv = t if pv is None else pv + t
            acc_ref[:, cols] = alpha[:, cols] * acc_ref[:, cols] + pv

    n_full = (c * tq) // tk

    def body(j, carry):
        step(j, False)
        return carry

    lax.fori_loop(0, n_full, body, 0)
    step(n_full, True)
    for h in range(MLA_HEADS):
        cols = slice(h * tq, (h + 1) * tq)
        out_ref[h] = acc_ref[:, cols] / l_ref[:, cols]


def _prompt_mla(qlat_t, qpe_t, ckvb, kpeb, ckv_t, batch, seq):
    tq = Q_BLOCK
    tk = min(512, seq)
    nq = seq // tq
    cols = MLA_HEADS * tq
    n = batch * seq
    return pl.pallas_call(
        functools.partial(_prompt_mla_kernel, tq=tq, tk=tk),
        grid=(batch, nq),
        in_specs=[
            pl.BlockSpec((MLA_HEADS, MLA_KV_LORA, tq), lambda b, c: (0, 0, b * nq + c)),
            pl.BlockSpec((MLA_HEADS, MLA_ROPE, tq), lambda b, c: (0, 0, b * nq + c)),
            pl.BlockSpec((seq, MLA_KV_LORA), lambda b, c: (b, 0)),
            pl.BlockSpec((seq, MLA_ROPE), lambda b, c: (b, 0)),
            pl.BlockSpec((seq // MOBA_BLOCK, MLA_KV_LORA, MOBA_BLOCK), lambda b, c: (b, 0, 0)),
        ],
        out_specs=pl.BlockSpec((MLA_HEADS, MLA_KV_LORA, tq), lambda b, c: (0, 0, b * nq + c)),
        out_shape=jax.ShapeDtypeStruct((MLA_HEADS, MLA_KV_LORA, n), F32),
        scratch_shapes=[pltpu.VMEM((MLA_HEADS // 2, MLA_KV_LORA, 2 * tq), BF16),
                        pltpu.VMEM((MLA_HEADS // 2, MLA_ROPE, 2 * tq), BF16),
                        pltpu.VMEM((1, cols), F32), pltpu.VMEM((1, cols), F32),
                        pltpu.VMEM((MLA_KV_LORA, cols), F32)],
        compiler_params=_params("parallel", "arbitrary"),
        name="prompt_mla",
    )(qlat_t, qpe_t, ckvb, kpeb, ckv_t)


def _prompt_moba_kernel(qt_ref, kb_ref, vt_ref, kmean_ref, bown_ref, bprev_ref, bfar_ref, out_ref,
                        sel_ref, m_ref, l_ref, acc_ref, *, tq, n_blk):
    c = pl.program_id(2)
    cols = MOBA_GROUP * tq
    own = (c * tq) // MOBA_BLOCK
    q_t = jnp.concatenate([qt_ref[u] for u in range(MOBA_GROUP)], axis=1)
    qb_t = (q_t * MOBA_SCALE).astype(BF16)

    s_blk = _dot(kmean_ref[0], q_t, precision=HIGHEST)
    blk = lax.broadcasted_iota(jnp.int32, (n_blk, cols), 0)
    s_blk = jnp.where(blk < own, s_blk, NEG_INF)
    sel_ref[...] = jnp.where(blk < own, _top_mask(s_blk, MOBA_TOPK, 0), 0.0)

    _softmax_init(m_ref, l_ref, acc_ref)

    last_group = own // BLOCKS_PER_STEP

    def group_step(jj, near):
        first = jj * BLOCKS_PER_STEP
        start = pl.multiple_of(first * MOBA_BLOCK, BLOCKS_PER_STEP * MOBA_BLOCK)
        s_all = _dot(kb_ref[0, pl.ds(start, BLOCKS_PER_STEP * MOBA_BLOCK), :], qb_t)
        parts = []
        for u in range(BLOCKS_PER_STEP):
            j = first + u
            s = s_all[u * MOBA_BLOCK:(u + 1) * MOBA_BLOCK]
            picked = sel_ref[pl.ds(j, 1), :]
            if near:
                bias = jnp.where(j == own, bown_ref[0, 0], jnp.where(j == own - 1, bprev_ref[0, 0], bfar_ref[0]))
                picked = jnp.where(j == own, 1.0, picked)
                parts.append(jnp.where(picked > 0.5, s + bias, NEG_INF))
            else:
                parts.append(s + jnp.where(picked > 0.5, bfar_ref[0], NEG_INF))
        m_old = m_ref[...]
        m_new = m_old
        for s in parts:
            m_new = jnp.maximum(m_new, jnp.max(s, axis=0, keepdims=True))
        alpha = jnp.exp(m_old - m_new)
        l_new = alpha * l_ref[...]
        pv = None
        for u in range(BLOCKS_PER_STEP):
            p = jnp.exp(parts[u] - m_new)
            l_new = l_new + jnp.sum(p, axis=0, keepdims=True)
            t = _dot(vt_ref[0, first + u], p.astype(BF16))
            pv = t if pv is None else pv + t
        l_ref[...] = l_new
        m_ref[...] = m_new
        acc_ref[...] = alpha * acc_ref[...] + pv

    def far(jj, carry):
        group_step(jj, False)
        return carry

    lax.fori_loop(0, last_group - 1, far, 0)

    @pl.when(last_group >= 1)
    def _():
        group_step(last_group - 1, True)

    group_step(last_group, True)
    out = acc_ref[...] / l_ref[...]
    for u in range(MOBA_GROUP):
        out_ref[u] = out[:, u * tq:(u + 1) * tq]


def _prompt_moba(qm_t, kb, vb_t, kmean_g, bias_own, bias_prev, bias_far, batch, seq):
    tq = Q_BLOCK
    nq = seq // tq
    n_blk = seq // MOBA_BLOCK
    cols = MOBA_GROUP * tq
    per_block = MOBA_BLOCK // tq
    n = batch * seq
    return pl.pallas_call(
        functools.partial(_prompt_moba_kernel, tq=tq, n_blk=n_blk),
        grid=(batch, MOBA_KV_HEADS, nq),
        in_specs=[
            pl.BlockSpec((MOBA_GROUP, MOBA_HEAD_DIM, tq), lambda b, g, c: (g, 0, b * nq + c)),
            pl.BlockSpec((1, seq, MOBA_HEAD_DIM), lambda b, g, c: (g, b, 0)),
            pl.BlockSpec((1, n_blk, MOBA_HEAD_DIM, MOBA_BLOCK), lambda b, g, c: (g, b, 0, 0)),
            pl.BlockSpec((1, n_blk, MOBA_HEAD_DIM), lambda b, g, c: (g, b, 0)),
            pl.BlockSpec((1, 1, MOBA_BLOCK, cols), lambda b, g, c: (g, c % per_block, 0, 0)),
            pl.BlockSpec((1, 1, MOBA_BLOCK, cols), lambda b, g, c: (g, c % per_block, 0, 0)),
            pl.BlockSpec((1, 1, cols), lambda b, g, c: (g, 0, 0)),
        ],
        out_specs=pl.BlockSpec((MOBA_GROUP, MOBA_HEAD_DIM, tq), lambda b, g, c: (g, 0, b * nq + c)),
        out_shape=jax.ShapeDtypeStruct((MOBA_HEADS, MOBA_HEAD_DIM, n), F32),
        scratch_shapes=[pltpu.VMEM((n_blk, cols), F32), pltpu.VMEM((1, cols), F32), pltpu.VMEM((1, cols), F32),
                        pltpu.VMEM((MOBA_HEAD_DIM, cols), F32)],
        compiler_params=_params("parallel", "parallel", "arbitrary"),
        name="prompt_moba",
    )(qm_t, kb, vb_t, kmean_g, bias_own, bias_prev, bias_far)


def _page_specs(block_shape, group, n_par=1):
    nd = len(block_shape)

    def make(u, k):
        def index_map(s, i, pt):
            return (0, pt[s * n_par + u, i * group + k]) + (0,) * (nd - 2)
        return pl.BlockSpec(block_shape, index_map)

    return [make(u, k) for u in range(n_par) for k in range(group)]


def _block_mean_kernel(pt_ref, *refs, group, n_blk):
    k_refs, out_ref = refs[:group], refs[group]
    i = pl.program_id(1)
    ppb = MOBA_BLOCK // PAGE_SIZE

    @pl.when(i == 0)
    def _():
        out_ref[...] = jnp.zeros(out_ref.shape, F32)

    col = lax.broadcasted_iota(jnp.int32, (MOBA_HEAD_DIM, n_blk), 1)
    for g in range(MOBA_KV_HEADS):
        cur = out_ref[0, g]
        for jb in range(group // ppb):
            acc = k_refs[jb * ppb][0, 0, g]
            for p in range(1, ppb):
                acc = acc + k_refs[jb * ppb + p][0, 0, g]
            mean = jnp.sum(acc, axis=1, keepdims=True) * (1.0 / MOBA_BLOCK)
            cur = jnp.where(col == i * (group // ppb) + jb, mean, cur)
        out_ref[0, g] = cur


def _block_means(page_table, cache_kt, group):
    n_seq, n_pages = page_table.shape
    n_blk = n_pages // (MOBA_BLOCK // PAGE_SIZE)
    page = (1, 1, MOBA_KV_HEADS, MOBA_HEAD_DIM, PAGE_SIZE)
    return pl.pallas_call(
        functools.partial(_block_mean_kernel, group=group, n_blk=n_blk),
        grid_spec=pltpu.PrefetchScalarGridSpec(
            num_scalar_prefetch=1,
            grid=(n_seq, n_pages // group),
            in_specs=_page_specs(page, group),
            out_specs=pl.BlockSpec((1, MOBA_KV_HEADS, MOBA_HEAD_DIM, n_blk), lambda s, i, pt: (s, 0, 0, 0)),
        ),
        out_shape=jax.ShapeDtypeStruct((n_seq, MOBA_KV_HEADS, MOBA_HEAD_DIM, n_blk), F32),
        compiler_params=_params("parallel", "arbitrary"),
        name="sample_block_means",
    )(page_table, *([cache_kt] * group))


def _sample_moba_kernel(pt_ref, *refs, group, ds, n_par):
    n_pg = n_par * group
    k_refs, v_refs = refs[:n_pg], refs[n_pg:2 * n_pg]
    (qm_ref, bm_ref, knew_ref, vnew_ref, expand_ref, blast_ref, bfar_ref, bnew_ref,
     out_ref, sel_ref, m_ref, l_ref, acc_ref) = refs[2 * n_pg:]
    i = pl.program_id(1)
    last = pl.num_programs(1) - 1
    rows = MOBA_GROUP * ds
    units = [(u, g) for u in range(n_par) for g in range(MOBA_KV_HEADS)]

    def q_of(u, g):
        q = qm_ref[g * MOBA_GROUP:(g + 1) * MOBA_GROUP, u * ds:(u + 1) * ds, :]
        return q.reshape(rows, MOBA_HEAD_DIM)

    def pages_t(page_refs, u, g):
        pages = [page_refs[u * group + k][0, 0, g] for k in range(group)]
        return jnp.concatenate(pages, axis=1).astype(BF16)

    def unit_rows(x, idx):
        return x[idx * rows:(idx + 1) * rows]

    @pl.when(i == 0)
    def _():
        _softmax_init(m_ref, l_ref, acc_ref)
        for idx, (u, g) in enumerate(units):
            s_blk = _dot(q_of(u, g), bm_ref[u, g], precision=HIGHEST)
            sel_ref[idx * rows:(idx + 1) * rows] = _top_mask(s_blk, MOBA_TOPK, 1)

    qb = [(q_of(u, g) * MOBA_SCALE).astype(BF16) for u, g in units]
    picked = _dot(sel_ref[...].astype(BF16), expand_ref[...])
    s = jnp.concatenate([_dot(qb[idx], pages_t(k_refs, u, g)) for idx, (u, g) in enumerate(units)], axis=0)
    bias = jnp.where(i == last, blast_ref[...], bfar_ref[...])
    s = jnp.where(picked > 0.5, s + bias, NEG_INF)

    def pv(p):
        return jnp.concatenate([_dot_nt(unit_rows(p, idx), pages_t(v_refs, u, g))
                                for idx, (u, g) in enumerate(units)], axis=0)

    _softmax_update(s, pv, m_ref, l_ref, acc_ref)

    @pl.when(i == last)
    def _():
        def new_rows(ref, u, g):
            return ref[u * ds:(u + 1) * ds, g * MOBA_HEAD_DIM:(g + 1) * MOBA_HEAD_DIM].astype(BF16)

        s_new = jnp.concatenate([_dot_nt(qb[idx], new_rows(knew_ref, u, g))
                                 for idx, (u, g) in enumerate(units)], axis=0) + bnew_ref[...]

        def pv_new(p):
            return jnp.concatenate([_dot(unit_rows(p, idx), new_rows(vnew_ref, u, g))
                                    for idx, (u, g) in enumerate(units)], axis=0)

        _softmax_update(s_new, pv_new, m_ref, l_ref, acc_ref)
        out = acc_ref[...] / l_ref[...]
        for idx, (u, g) in enumerate(units):
            out_ref[g * MOBA_GROUP:(g + 1) * MOBA_GROUP, u * ds:(u + 1) * ds, :] = (
                unit_rows(out, idx).reshape(MOBA_GROUP, ds, MOBA_HEAD_DIM))


def _sample_moba(page_table, cache_kt, cache_vt, qm, blk_mean_t, km, vm, expand, bias_last, bias_far, bias_new,
                 ds, group, n_par):
    n_seq, n_pages = page_table.shape
    n_blk = blk_mean_t.shape[3]
    all_rows = n_par * MOBA_KV_HEADS * MOBA_GROUP * ds
    keys = group * PAGE_SIZE
    page = (1, 1, MOBA_KV_HEADS, MOBA_HEAD_DIM, PAGE_SIZE)
    tok = n_par * ds
    const2 = lambda a: pl.BlockSpec(a.shape, lambda s, i, pt: (0, 0))
    return pl.pallas_call(
        functools.partial(_sample_moba_kernel, group=group, ds=ds, n_par=n_par),
        grid_spec=pltpu.PrefetchScalarGridSpec(
            num_scalar_prefetch=1,
            grid=(n_seq // n_par, n_pages // group),
            in_specs=_page_specs(page, group, n_par) + _page_specs(page, group, n_par) + [
                pl.BlockSpec((MOBA_HEADS, tok, MOBA_HEAD_DIM), lambda s, i, pt: (0, s, 0)),
                pl.BlockSpec((n_par, MOBA_KV_HEADS, MOBA_HEAD_DIM, n_blk), lambda s, i, pt: (s, 0, 0, 0)),
                pl.BlockSpec((tok, KV_W), lambda s, i, pt: (s, 0)),
                pl.BlockSpec((tok, KV_W), lambda s, i, pt: (s, 0)),
                pl.BlockSpec((n_blk, keys), lambda s, i, pt: (0, i)),
                const2(bias_last), const2(bias_far), const2(bias_new),
            ],
            out_specs=pl.BlockSpec((MOBA_HEADS, tok, MOBA_HEAD_DIM), lambda s, i, pt: (0, s, 0)),
            scratch_shapes=[pltpu.VMEM((all_rows, n_blk), F32),
                            pltpu.VMEM((all_rows, 1), F32),
                            pltpu.VMEM((all_rows, 1), F32),
                            pltpu.VMEM((all_rows, MOBA_HEAD_DIM), F32)],
        ),
        out_shape=jax.ShapeDtypeStruct((MOBA_HEADS, n_seq * ds, MOBA_HEAD_DIM), F32),
        compiler_params=_params("parallel", "arbitrary"),
        name="sample_moba",
    )(page_table, *([cache_kt] * (n_par * group)), *([cache_vt] * (n_par * group)), qm, blk_mean_t, km, vm,
      expand, bias_last, bias_far, bias_new)


def _sample_mla_kernel(pt_ref, *refs, group, ds, n_par):
    n_pg = n_par * group
    c_refs, p_refs = refs[:n_pg], refs[n_pg:2 * n_pg]
    qlat_ref, qpe_ref, cnew_ref, pnew_ref, out_ref, m_ref, l_ref, acc_ref = refs[2 * n_pg:]
    i = pl.program_id(1)
    last = pl.num_programs(1) - 1
    rows = MLA_HEADS * ds
    seqs = range(n_par)
    q1 = [qlat_ref[:, u * ds:(u + 1) * ds, :].reshape(rows, MLA_KV_LORA).astype(BF16) for u in seqs]
    q2 = [qpe_ref[:, u * ds:(u + 1) * ds, :].reshape(rows, MLA_ROPE).astype(BF16) for u in seqs]

    @pl.when(i == 0)
    def _():
        _softmax_init(m_ref, l_ref, acc_ref)

    k1 = [jnp.concatenate([c_refs[u * group + k][0, 0] for k in range(group)], axis=0).astype(BF16)
          for u in seqs]
    k2_t = [jnp.concatenate([p_refs[u * group + k][0, 0] for k in range(group)], axis=1).astype(BF16)
            for u in seqs]
    s = jnp.concatenate([_dot_nt(q1[u], k1[u]) + _dot(q2[u], k2_t[u]) for u in seqs], axis=0)
    _softmax_update(s, lambda p: jnp.concatenate([_dot(p[u * rows:(u + 1) * rows], k1[u]) for u in seqs], axis=0),
                    m_ref, l_ref, acc_ref)

    @pl.when(i == last)
    def _():
        c_new = [cnew_ref[u * ds:(u + 1) * ds].astype(BF16) for u in seqs]
        p_new = [pnew_ref[u * ds:(u + 1) * ds].astype(BF16) for u in seqs]
        s_new = jnp.concatenate([_dot_nt(q1[u], c_new[u]) + _dot_nt(q2[u], p_new[u]) for u in seqs], axis=0)
        tq = jnp.bitwise_and(lax.broadcasted_iota(jnp.int32, s_new.shape, 0), ds - 1)
        tk = lax.broadcasted_iota(jnp.int32, s_new.shape, 1)
        _softmax_update(jnp.where(tk <= tq, s_new, NEG_INF),
                        lambda p: jnp.concatenate([_dot(p[u * rows:(u + 1) * rows], c_new[u]) for u in seqs], axis=0),
                        m_ref, l_ref, acc_ref)
        out = acc_ref[...] / l_ref[...]
        for u in seqs:
            out_ref[:, u * ds:(u + 1) * ds, :] = out[u * rows:(u + 1) * rows].reshape(MLA_HEADS, ds, MLA_KV_LORA)


def _sample_mla(page_table, cache_ckv, cache_kpe_t, qlat, qpe, ckv, kpe, ds, group, n_par):
    n_seq, n_pages = page_table.shape
    all_rows = n_par * MLA_HEADS * ds
    tok = n_par * ds
    return pl.pallas_call(
        functools.partial(_sample_mla_kernel, group=group, ds=ds, n_par=n_par),
        grid_spec=pltpu.PrefetchScalarGridSpec(
            num_scalar_prefetch=1,
            grid=(n_seq // n_par, n_pages // group),
            in_specs=_page_specs((1, 1, PAGE_SIZE, MLA_KV_LORA), group, n_par)
            + _page_specs((1, 1, MLA_ROPE, PAGE_SIZE), group, n_par) + [
                pl.BlockSpec((MLA_HEADS, tok, MLA_KV_LORA), lambda s, i, pt: (0, s, 0)),
                pl.BlockSpec((MLA_HEADS, tok, MLA_ROPE), lambda s, i, pt: (0, s, 0)),
                pl.BlockSpec((tok, MLA_KV_LORA), lambda s, i, pt: (s, 0)),
                pl.BlockSpec((tok, MLA_ROPE), lambda s, i, pt: (s, 0)),
            ],
            out_specs=pl.BlockSpec((MLA_HEADS, tok, MLA_KV_LORA), lambda s, i, pt: (0, s, 0)),
            scratch_shapes=[pltpu.VMEM((all_rows, 1), F32), pltpu.VMEM((all_rows, 1), F32),
                            pltpu.VMEM((all_rows, MLA_KV_LORA), F32)],
        ),
        out_shape=jax.ShapeDtypeStruct((MLA_HEADS, n_seq * ds, MLA_KV_LORA), F32),
        compiler_params=_params("parallel", "arbitrary"),
        name="sample_mla",
    )(page_table, *([cache_ckv] * (n_par * group)), *([cache_kpe_t] * (n_par * group)), qlat, qpe, ckv, kpe)


def _merge_kernel(omoba_ref, olat_ref, gates_ref, x_ref, wuv_ref, wbm_ref, wbl_ref, wout_ref, g_ref, b_ref,
                  h_ref, hb_ref, *, alpha):
    y_moba = None
    y_mla = None
    for h in range(MOBA_HEADS):
        t = _dot(wbm_ref[h], omoba_ref[h].astype(BF16))
        y_moba = t if y_moba is None else y_moba + t
    for h in range(MLA_HEADS):
        o = _dot(wuv_ref[h], olat_ref[h].astype(BF16)).astype(BF16)
        t = _dot(wbl_ref[h], o)
        y_mla = t if y_mla is None else y_mla + t
    merged = gates_ref[:D_MODEL] * y_moba + gates_ref[D_MODEL:] * y_mla
    mix = _dot(wout_ref[...], merged.astype(BF16))
    h = _layer_norm(alpha * x_ref[...] + mix.T, g_ref[...], b_ref[...])
    h_ref[...] = h
    hb_ref[...] = h.astype(BF16)


def _merge(omoba_t, olat_t, gates_t, x, wuv_t, wbm_t, wbl_t, wout_t, ln_g, ln_b, alpha, name):
    n = x.shape[0]
    tb = 256
    row = lambda w: pl.BlockSpec((tb, w), lambda i: (i, 0))
    full = lambda a: pl.BlockSpec(a.shape, lambda i: (0,) * a.ndim)
    return pl.pallas_call(
        functools.partial(_merge_kernel, alpha=alpha),
        grid=(n // tb,),
        in_specs=[pl.BlockSpec((MOBA_HEADS, MOBA_HEAD_DIM, tb), lambda i: (0, 0, i)),
                  pl.BlockSpec((MLA_HEADS, MLA_KV_LORA, tb), lambda i: (0, 0, i)),
                  pl.BlockSpec((2 * D_MODEL, tb), lambda i: (0, i)),
                  row(D_MODEL), full(wuv_t), full(wbm_t), full(wbl_t), full(wout_t), full(ln_g), full(ln_b)],
        out_specs=(row(D_MODEL), row(D_MODEL)),
        out_shape=(jax.ShapeDtypeStruct((n, D_MODEL), F32), jax.ShapeDtypeStruct((n, D_MODEL), BF16)),
        compiler_params=_params("parallel"),
        name=name,
    )(omoba_t, olat_t, gates_t, x, wuv_t, wbm_t, wbl_t, wout_t, ln_g, ln_b)


def _peer_pairs():
    return [(a, b) for a in range(PEER_TOPK) for b in range(PEER_TOPK) if (a + 1) * (b + 1) <= PEER_TOPK]


def _top_values(s, n):
    vals = []
    for _ in range(n):
        m = jnp.max(s, axis=0, keepdims=True)
        vals.append(m)
        s = jnp.where(s == m, KNOCKED_OUT, s)
    return vals


def _peer_select_kernel(hb_ref, wq_ref, k1_ref, k2_ref, a_ref, c_ref, b_ref, s2_ref):
    q = _dot(hb_ref[...], wq_ref[...])
    half = PEER_KEY_DIM // 2
    pairs = _peer_pairs()
    for h in range(PEER_HEADS):
        q1 = q[:, h * PEER_KEY_DIM:h * PEER_KEY_DIM + half]
        q2 = q[:, h * PEER_KEY_DIM + half:(h + 1) * PEER_KEY_DIM]
        s1 = _dot_nt(k1_ref[h], q1, precision=HIGHEST)
        s2 = _dot_nt(k2_ref[h], q2, precision=HIGHEST)
        v1 = _top_values(s1, PEER_TOPK)
        v2 = _top_values(s2, PEER_TOPK)
        cand = jnp.concatenate([v1[a] + v2[b] for a, b in pairs], axis=0)
        thr = _top_values(cand, PEER_TOPK)[-1]
        top = v1[0] + v2[0]
        z = jnp.sum(jnp.where(cand >= thr, jnp.exp(cand - top), 0.0), axis=0, keepdims=True)
        cut = jnp.full(s1.shape, -KNOCKED_OUT, F32)
        for a in range(PEER_TOPK):
            lim = None
            for b in range(PEER_TOPK // (a + 1)):
                t = jnp.where(v1[a] + v2[b] >= thr, v2[b], -KNOCKED_OUT)
                lim = t if lim is None else jnp.minimum(lim, t)
            cut = jnp.where(s1 == v1[a], lim, cut)
        a_ref[h] = jnp.exp(s1 - v1[0]) / z
        c_ref[h] = cut
        b_ref[h] = jnp.exp(s2 - v2[0])
        s2_ref[h] = s2


def _peer_select(hb, wq, k1, k2, name):
    n = hb.shape[0]
    tb = 256
    full = lambda a: pl.BlockSpec(a.shape, lambda i: (0,) * a.ndim)
    sel = pl.BlockSpec((PEER_HEADS, PEER_NKEYS, tb), lambda i: (0, 0, i))
    shape = jax.ShapeDtypeStruct((PEER_HEADS, PEER_NKEYS, n), F32)
    return pl.pallas_call(
        _peer_select_kernel,
        grid=(n // tb,),
        in_specs=[pl.BlockSpec((tb, D_MODEL), lambda i: (i, 0)), full(wq), full(k1), full(k2)],
        out_specs=(sel, sel, sel, sel),
        out_shape=(shape, shape, shape, shape),
        compiler_params=_params("parallel"),
        name=name,
    )(hb, wq, k1, k2)


def _peer_expert_kernel(hb_ref, h_ref, u_ref, vt_ref, a_ref, c_ref, b_ref, s2_ref, g_ref, bias_ref,
                        out_ref, acc_ref, gw_ref, *, alpha, rows_per_step):
    e = pl.program_id(1)

    @pl.when(e == 0)
    def _():
        acc_ref[...] = jnp.zeros(acc_ref.shape, F32)

    sub = PEER_SUB_ROWS * PEER_NKEYS
    n_sub = rows_per_step // PEER_SUB_ROWS
    acts = [_dot_nt(u_ref[sb * sub:(sb + 1) * sub, :], hb_ref[...]) for sb in range(n_sub)]
    partial = None
    for sb in range(n_sub):
        for rr in range(PEER_SUB_ROWS):
            r = sb * PEER_SUB_ROWS + rr
            i1 = e * rows_per_step + r
            a_full = [a_ref[h, pl.ds(i1, 1), :] for h in range(PEER_HEADS)]
            c_full = [c_ref[h, pl.ds(i1, 1), :] for h in range(PEER_HEADS)]
            for lc in range(hb_ref.shape[0] // LANES):
                lanes = slice(lc * LANES, (lc + 1) * LANES)
                a_rows = [jnp.broadcast_to(a_full[h][:, lanes], (SUBLANES, LANES)) for h in range(PEER_HEADS)]
                c_rows = [jnp.broadcast_to(c_full[h][:, lanes], (SUBLANES, LANES)) for h in range(PEER_HEADS)]
                for sc in range(PEER_NKEYS // (2 * SUBLANES)):
                    halves = []
                    for half in range(2):
                        k0 = (2 * sc + half) * SUBLANES
                        keys = slice(k0, k0 + SUBLANES)
                        w = None
                        for h in range(PEER_HEADS):
                            t = jnp.where(s2_ref[h, keys, lanes] >= c_rows[h], a_rows[h] * b_ref[h, keys, lanes], 0.0)
                            w = t if w is None else w + t
                        x = acts[sb][rr * PEER_NKEYS + k0:rr * PEER_NKEYS + k0 + SUBLANES, lanes]
                        halves.append(w * (0.5 * x * (1.0 + lax.erf(x * (2.0 ** -0.5)))))
                    rows = slice(r * PEER_NKEYS + 2 * sc * SUBLANES, r * PEER_NKEYS + 2 * (sc + 1) * SUBLANES)
                    gw_ref[rows, lanes] = jnp.concatenate(halves, axis=0).astype(BF16)
        t = _dot(vt_ref[:, sb * sub:(sb + 1) * sub], gw_ref[sb * sub:(sb + 1) * sub, :])
        partial = t if partial is None else partial + t
    acc_ref[...] += partial

    @pl.when(e == pl.num_programs(1) - 1)
    def _():
        f = acc_ref[...].T
        out_ref[...] = _layer_norm(alpha * h_ref[...] + f, g_ref[...], bias_ref[...])


def _peer_expert(hb, h, u_b, vt_b, sel, ln_g, ln_b, alpha, name):
    n = hb.shape[0]
    tb = 512
    rows_per_step = PEER_ROWS_PER_STEP
    n_exp = u_b.shape[0]
    et = rows_per_step * PEER_NKEYS
    sel_spec = pl.BlockSpec((PEER_HEADS, PEER_NKEYS, tb), lambda i, e: (0, 0, i))
    vec = pl.BlockSpec((1, D_MODEL), lambda i, e: (0, 0))
    return pl.pallas_call(
        functools.partial(_peer_expert_kernel, alpha=alpha, rows_per_step=rows_per_step),
        grid=(n // tb, n_exp // et),
        in_specs=[pl.BlockSpec((tb, D_MODEL), lambda i, e: (i, 0)),
                  pl.BlockSpec((tb, D_MODEL), lambda i, e: (i, 0)),
                  pl.BlockSpec((et, D_MODEL), lambda i, e: (e, 0)),
                  pl.BlockSpec((D_MODEL, et), lambda i, e: (0, e)),
                  sel_spec, sel_spec, sel_spec, sel_spec, vec, vec],
        out_specs=pl.BlockSpec((tb, D_MODEL), lambda i, e: (i, 0)),
        out_shape=jax.ShapeDtypeStruct((n, D_MODEL), F32),
        scratch_shapes=[pltpu.VMEM((D_MODEL, tb), F32), pltpu.VMEM((et, tb), BF16)],
        compiler_params=_params("parallel", "arbitrary"),
        name=name,
    )(hb, h, u_b, vt_b, *sel, ln_g, ln_b)


def _rel_bucket_static(dist):
    n = np.maximum(dist, 0)
    max_exact = REL_BUCKETS // 2
    nf = np.maximum(n, 1).astype(np.float32)
    large = max_exact + (np.log(nf / max_exact) / math.log(REL_MAX_DIST / max_exact)
                         * (REL_BUCKETS - max_exact)).astype(np.int32)
    return np.where(n < max_exact, n, np.minimum(large, REL_BUCKETS - 1)).astype(np.int32)


def _bias_tile(rel_bias, dist, causal, queries_on_lanes):
    lq, lk = dist.shape
    onehot = jax.nn.one_hot(jnp.asarray(_rel_bucket_static(dist).reshape(-1)), REL_BUCKETS, dtype=F32)
    tile = _dot(onehot, rel_bias.astype(F32), precision=HIGHEST).reshape(lq, lk, MOBA_HEADS)
    if causal:
        tile = jnp.where(jnp.asarray(dist >= 0)[:, :, None], tile, NEG_INF)
    tile = jnp.transpose(tile, (2, 0, 1)).reshape(MOBA_KV_HEADS, MOBA_GROUP * lq, lk)
    return jnp.transpose(tile, (0, 2, 1)) if queries_on_lanes else tile


def _rope_table(pos):
    half = MLA_ROPE // 2
    inv_freq = ROPE_THETA ** (-jnp.arange(half, dtype=F32) / half)
    ang = pos.astype(F32)[:, None] * inv_freq[None, :]
    cos, sin = jnp.cos(ang), jnp.sin(ang)
    cos_full = jnp.tile(jnp.concatenate([cos, cos], -1), (1, MLA_HEADS))
    sin_signed = jnp.tile(jnp.concatenate([-sin, sin], -1), (1, MLA_HEADS))
    return jnp.concatenate([cos_full, sin_signed], -1)


def _swap_halves(w):
    half = w.shape[-1] // 2
    return jnp.concatenate([w[..., half:], w[..., :half]], -1)


def _largest_divisor(n, cap):
    d = min(n, cap)
    while n % d:
        d -= 1
    return d


def kernel(x_prompt, x_sample, cache_moba_k, cache_moba_v, cache_mla_ckv, cache_mla_kpe, page_table, w_in, b_gate, mla_q_norm, mla_w_uq, mla_kv_norm, mla_w_uk, mla_w_uv, rel_bias, w_br_moba, w_br_mla, w_out, ln1_g, ln1_b, peer_wq, peer_k1, peer_k2, peer_u, peer_v, ln2_g, ln2_b):
    depth = w_in.shape[0]
    assert depth == 1, "single-layer step"
    batch, seq, _ = x_prompt.shape
    n_seq, ds, _ = x_sample.shape
    n_pages = page_table.shape[1]
    past = n_pages * PAGE_SIZE
    n_prompt = batch * seq
    n_sample = n_seq * ds
    assert seq % 512 == 0 and seq // MOBA_BLOCK >= MOBA_TOPK
    assert past % MOBA_BLOCK == 0 and past // MOBA_BLOCK >= MOBA_TOPK
    assert n_prompt % 512 == 0 and n_sample % 512 == 0 and ds == 8
    assert (seq // MOBA_BLOCK) % BLOCKS_PER_STEP == 0 and n_seq % SEQS_PER_STEP == 0
    alpha = (2 * depth) ** 0.25
    layer = 0

    w = w_in[layer]
    c0 = MOBA_W + 2 * KV_W
    lat_w = MLA_Q_LORA + MLA_KV_LORA
    wq = w[:, :MOBA_W].astype(BF16)
    wkv = w[:, MOBA_W:c0].astype(BF16)
    wv_t = w[:, MOBA_W + KV_W:c0].astype(BF16).T
    kr_w = w[:, c0 + lat_w:c0 + lat_w + MLA_ROPE]
    pad = jnp.zeros((D_MODEL, 128 - 2 * MLA_ROPE), w.dtype)
    wlat = jnp.concatenate([w[:, c0:c0 + lat_w], kr_w, _swap_halves(kr_w), pad], -1).astype(BF16)
    wgate_t = w[:, c0 + lat_w + MLA_ROPE:].astype(BF16).T
    bgate_t = b_gate[layer].reshape(-1, 1)
    uq = mla_w_uq[layer]
    uq_pe = uq[..., MLA_NOPE:]
    wuq = jnp.concatenate([uq[..., :MLA_NOPE].reshape(MLA_Q_LORA, -1), uq_pe.reshape(MLA_Q_LORA, -1),
                           _swap_halves(uq_pe).reshape(MLA_Q_LORA, -1)], -1).astype(BF16)
    wuk = jnp.transpose(mla_w_uk[layer], (1, 2, 0)).astype(BF16)
    wuk_t = jnp.transpose(mla_w_uk[layer], (1, 0, 2)).astype(BF16)
    wuv_t = jnp.transpose(mla_w_uv[layer], (1, 2, 0)).astype(BF16)
    wbm_t = jnp.transpose(w_br_moba[layer].reshape(MOBA_HEADS, MOBA_HEAD_DIM, D_MODEL), (0, 2, 1)).astype(BF16)
    wbl_t = jnp.transpose(w_br_mla[layer].reshape(MLA_HEADS, MLA_V, D_MODEL), (0, 2, 1)).astype(BF16)
    wout_t = w_out[layer].astype(BF16).T
    peer_q = peer_wq[layer].astype(BF16)
    u_b = peer_u[layer].astype(BF16)
    vt_b = peer_v[layer].astype(BF16).T
    vec = lambda a: a[layer].reshape(1, -1)
    shared = (wkv, wlat, wgate_t, bgate_t, vec(mla_q_norm), vec(mla_kv_norm))

    tab_p = _rope_table(jnp.tile(jnp.arange(seq), batch))
    tab_s = _rope_table(jnp.tile(past + jnp.arange(ds), n_seq))
    qi = np.arange(Q_BLOCK)[:, None]
    ki = np.arange(MOBA_BLOCK)[None, :]
    per_block = MOBA_BLOCK // Q_BLOCK
    bias_own = jnp.stack([_bias_tile(rel_bias, o * Q_BLOCK + qi - ki, True, True) for o in range(per_block)], 1)
    bias_prev = jnp.stack([_bias_tile(rel_bias, MOBA_BLOCK + o * Q_BLOCK + qi - ki, False, True)
                           for o in range(per_block)], 1)
    far = rel_bias[REL_BUCKETS - 1].astype(F32)
    bias_far = jnp.repeat(far, Q_BLOCK).reshape(MOBA_KV_HEADS, 1, -1)
    ti = np.arange(ds)[:, None]
    ppb = MOBA_BLOCK // PAGE_SIZE
    mean_group = _largest_divisor(n_pages // ppb, MEAN_PAGES_PER_STEP // ppb) * ppb
    group = _largest_divisor(n_pages // ppb, ATT_PAGES_PER_STEP // ppb) * ppb
    n_par = SEQS_PER_STEP
    keys = group * PAGE_SIZE
    per_seq = lambda a: jnp.tile(a.reshape(MOBA_HEADS * ds, -1), (n_par, 1))
    far_s = jnp.repeat(far, ds).reshape(MOBA_KV_HEADS, -1, 1)
    tail = _bias_tile(rel_bias, past + ti - (past - MOBA_BLOCK + ki), False, False)
    bias_last = per_seq(jnp.concatenate(
        [jnp.broadcast_to(far_s, far_s.shape[:2] + (keys - MOBA_BLOCK,)), tail], -1))
    bias_new = per_seq(_bias_tile(rel_bias, ti - np.arange(ds)[None, :], True, False))
    far_s = per_seq(far_s)
    n_blk_s = past // MOBA_BLOCK
    expand = jnp.asarray(np.repeat(np.eye(n_blk_s, dtype=np.float32), MOBA_BLOCK, axis=1), BF16)

    cache_kt = jnp.swapaxes(cache_moba_k, 3, 4)
    cache_vt = jnp.swapaxes(cache_moba_v, 3, 4)
    cache_kpe_t = jnp.swapaxes(cache_mla_kpe, 2, 3)

    xp = x_prompt.reshape(n_prompt, D_MODEL)
    (km_p, vm_p, ckv_p, kpe_p, gates_p, qm_t, kb, vb_t, kmean, ckvb, kpeb, ckv_t, qlat_t, qpe_t) = _project_prompt(
        xp, tab_p, tab_p.T, shared, wq.T, wv_t, wuq.T, wuk_t)
    kmean_g = jnp.transpose(kmean.reshape(-1, MOBA_KV_HEADS, MOBA_HEAD_DIM), (1, 0, 2))
    olat_p = _prompt_mla(qlat_t, qpe_t, ckvb, kpeb, ckv_t, batch, seq)
    omoba_p = _prompt_moba(qm_t, kb, vb_t, kmean_g, bias_own, bias_prev, bias_far, batch, seq)
    merge_w = (wuv_t, wbm_t, wbl_t, wout_t, vec(ln1_g), vec(ln1_b))
    h_p, hb_p = _merge(omoba_p, olat_p, gates_p, xp, *merge_w, alpha, "prompt_merge_ln1")
    sel_p = _peer_select(hb_p, peer_q, peer_k1[layer], peer_k2[layer], "prompt_peer_select")
    y_p = _peer_expert(hb_p, h_p, u_b, vt_b, sel_p, vec(ln2_g), vec(ln2_b), alpha, "prompt_peer_experts")

    xs = x_sample.reshape(n_sample, D_MODEL)
    km_s, vm_s, ckv_s, kpe_s, gates_s, qm, qlat, qpe = _project_sample(xs, tab_s, shared, wq, wuq, wuk)
    blk_mean_t = _block_means(page_table, cache_kt, mean_group)
    omoba_s = _sample_moba(page_table, cache_kt, cache_vt, qm, blk_mean_t, km_s, vm_s, expand, bias_last,
                           far_s, bias_new, ds, group, n_par)
    olat_s = _sample_mla(page_table, cache_mla_ckv, cache_kpe_t, qlat, qpe, ckv_s, kpe_s, ds, group, n_par)
    h_s, hb_s = _merge(jnp.swapaxes(omoba_s, 1, 2), jnp.swapaxes(olat_s, 1, 2), gates_s, xs, *merge_w, alpha,
                       "sample_merge_ln1")
    sel_s = _peer_select(hb_s, peer_q, peer_k1[layer], peer_k2[layer], "sample_peer_select")
    y_s = _peer_expert(hb_s, h_s, u_b, vt_b, sel_s, vec(ln2_g), vec(ln2_b), alpha, "sample_peer_experts")

    lead_p, lead_s = (depth, batch, seq), (depth, n_seq, ds)
    kv_tail = (MOBA_KV_HEADS, MOBA_HEAD_DIM)
    return (
        y_p.reshape(batch, seq, D_MODEL),
        y_s.reshape(n_seq, ds, D_MODEL),
        km_p.reshape(lead_p + kv_tail), vm_p.reshape(lead_p + kv_tail),
        ckv_p.reshape(lead_p + (MLA_KV_LORA,)), kpe_p.reshape(lead_p + (MLA_ROPE,)),
        km_s.reshape(lead_s + kv_tail), vm_s.reshape(lead_s + kv_tail),
        ckv_s.reshape(lead_s + (MLA_KV_LORA,)), kpe_s.reshape(lead_s + (MLA_ROPE,)),
    )
```

```python
import functools
import math

import numpy as np
import jax
import jax.numpy as jnp
from jax import lax
from jax.experimental import pallas as pl
from jax.experimental.pallas import tpu as pltpu

D_MODEL = 1024
PAGE_SIZE = 128
MOBA_HEADS = 8
MOBA_KV_HEADS = 4
MOBA_GROUP = MOBA_HEADS // MOBA_KV_HEADS
MOBA_HEAD_DIM = 64
MOBA_BLOCK = 256
MOBA_TOPK = 3
BLOCKS_PER_STEP = 4
MEAN_PAGES_PER_STEP = 32
ATT_PAGES_PER_STEP = 16
SEQS_PER_STEP = 2
MLA_HEADS = 8
MLA_NOPE = 64
MLA_ROPE = 32
MLA_V = 64
MLA_Q_LORA = 256
MLA_KV_LORA = 256
ROPE_THETA = 10000.0
REL_BUCKETS = 32
REL_MAX_DIST = 128
PEER_HEADS = 8
PEER_NKEYS = 128
PEER_KEY_DIM = 128
PEER_TOPK = 16
PEER_ROWS_PER_STEP = 4
PEER_SUB_ROWS = 2
Q_BLOCK = 128
LN_EPS = 1e-5
RMS_EPS = 1e-6
NEG_INF = -1e30
KNOCKED_OUT = -3e38

KV_W = MOBA_KV_HEADS * MOBA_HEAD_DIM
MOBA_W = MOBA_HEADS * MOBA_HEAD_DIM
NOPE_W = MLA_HEADS * MLA_NOPE
PE_W = MLA_HEADS * MLA_ROPE
MLA_SCALE = (MLA_NOPE + MLA_ROPE) ** -0.5
MOBA_SCALE = MOBA_HEAD_DIM ** -0.5

VMEM_LIMIT_BYTES = 56 * 1024 * 1024
SUBLANES, LANES = 8, 128
BF16_SUBLANES = 2 * SUBLANES

BF16 = jnp.bfloat16
F32 = jnp.float32
HIGHEST = lax.Precision.HIGHEST


def _dot(a, b, precision=None):
    return jnp.dot(a, b, precision=precision, preferred_element_type=F32)


def _dot_nt(a, b, precision=None):
    return lax.dot_general(a, b, (((1,), (1,)), ((), ())), precision=precision,
                           preferred_element_type=F32)


def _params(*sem):
    return pltpu.CompilerParams(dimension_semantics=sem, vmem_limit_bytes=VMEM_LIMIT_BYTES)


def _rms(x, g):
    return x * lax.rsqrt(jnp.mean(x * x, axis=-1, keepdims=True) + RMS_EPS) * g


def _layer_norm(x, g, b):
    mu = jnp.mean(x, axis=-1, keepdims=True)
    xc = x - mu
    var = jnp.mean(xc * xc, axis=-1, keepdims=True)
    return xc * lax.rsqrt(var + LN_EPS) * g + b


def _sigmoid(x):
    return 1.0 / (1.0 + jnp.exp(-x))


def _softmax_init(m_ref, l_ref, acc_ref):
    m_ref[...] = jnp.full(m_ref.shape, NEG_INF, F32)
    l_ref[...] = jnp.zeros(l_ref.shape, F32)
    acc_ref[...] = jnp.zeros(acc_ref.shape, F32)


def _softmax_update(s, pv, m_ref, l_ref, acc_ref):
    m_old = m_ref[...]
    m_new = jnp.maximum(m_old, jnp.max(s, axis=-1, keepdims=True))
    alpha = jnp.exp(m_old - m_new)
    p = jnp.exp(s - m_new)
    l_ref[...] = alpha * l_ref[...] + jnp.sum(p, axis=-1, keepdims=True)
    acc_ref[...] = alpha * acc_ref[...] + pv(p.astype(BF16))
    m_ref[...] = m_new


def _top_mask(s, n_sel, axis):
    pos = lax.broadcasted_iota(jnp.int32, s.shape, axis)
    sel = jnp.zeros(s.shape, F32)
    for _ in range(n_sel):
        m = jnp.max(s, axis=axis, keepdims=True)
        idx = jnp.min(jnp.where(s == m, pos, s.shape[axis]), axis=axis, keepdims=True)
        pick = pos == idx
        sel = jnp.where(pick, 1.0, sel)
        s = jnp.where(pick, KNOCKED_OUT, s)
    return sel


def _proj_common(x_ref, tab_ref, wkv_ref, wlat_ref, wgate_ref, bgate_ref, qn_ref, kvn_ref,
                 km_ref, vm_ref, ckv_ref, kpe_ref, gates_ref):
    xb = x_ref[...].astype(BF16)
    kv = _dot(xb, wkv_ref[...])
    km, vm = kv[:, :KV_W], kv[:, KV_W:]
    km_ref[...] = km
    vm_ref[...] = vm
    lat = _dot(xb, wlat_ref[...])
    cq = lat[:, :MLA_Q_LORA]
    ckv = lat[:, MLA_Q_LORA:MLA_Q_LORA + MLA_KV_LORA]
    base = MLA_Q_LORA + MLA_KV_LORA
    kr = lat[:, base:base + MLA_ROPE]
    kr_swapped = lat[:, base + MLA_ROPE:base + 2 * MLA_ROPE]
    ckv_n = _rms(ckv, kvn_ref[...])
    ckv_ref[...] = ckv_n
    kpe = kr * tab_ref[:, :MLA_ROPE] + kr_swapped * tab_ref[:, PE_W:PE_W + MLA_ROPE]
    kpe_ref[...] = kpe
    gates_ref[...] = _sigmoid(_dot_nt(wgate_ref[...], xb) + bgate_ref[...])
    cqn = _rms(cq, qn_ref[...]).astype(BF16)
    return xb, km, ckv_n, kpe, cqn


def _proj_prompt_kernel(x_ref, tab_ref, tabt_ref, wkv_ref, wlat_ref, wgate_ref, bgate_ref, qn_ref, kvn_ref,
                        wqt_ref, wvt_ref, wuqt_ref, wukt_ref,
                        km_ref, vm_ref, ckv_ref, kpe_ref, gates_ref,
                        qmt_ref, kb_ref, vbt_ref, kmean_ref, ckvb_ref, kpeb_ref, ckvt_ref, qlatt_ref, qpet_ref):
    xb, km, ckv_n, kpe, cqn = _proj_common(x_ref, tab_ref, wkv_ref, wlat_ref, wgate_ref, bgate_ref, qn_ref,
                                           kvn_ref, km_ref, vm_ref, ckv_ref, kpe_ref, gates_ref)
    q_t = _dot_nt(wqt_ref[...], xb)
    for h in range(MOBA_HEADS):
        qmt_ref[h] = q_t[h * MOBA_HEAD_DIM:(h + 1) * MOBA_HEAD_DIM]
    v_t = _dot_nt(wvt_ref[...], xb)
    for g in range(MOBA_KV_HEADS):
        kb_ref[g] = km[:, g * MOBA_HEAD_DIM:(g + 1) * MOBA_HEAD_DIM].astype(BF16)
        vbt_ref[g, 0] = v_t[g * MOBA_HEAD_DIM:(g + 1) * MOBA_HEAD_DIM].astype(BF16)
    kmean_ref[0] = jnp.mean(km, axis=0, keepdims=True)
    ckvb_ref[...] = ckv_n.astype(BF16)
    kpeb_ref[...] = kpe.astype(BF16)
    ckvt_ref[0] = ckv_n.T.astype(BF16)
    qq_t = _dot_nt(wuqt_ref[...], cqn)
    qpe_t = (qq_t[NOPE_W:NOPE_W + PE_W] * tabt_ref[:PE_W] + qq_t[NOPE_W + PE_W:] * tabt_ref[PE_W:]) * MLA_SCALE
    for h in range(MLA_HEADS):
        qpet_ref[h] = qpe_t[h * MLA_ROPE:(h + 1) * MLA_ROPE].astype(BF16)
        q_nope_t = qq_t[h * MLA_NOPE:(h + 1) * MLA_NOPE].astype(BF16)
        qlatt_ref[h] = (_dot(wukt_ref[h], q_nope_t) * MLA_SCALE).astype(BF16)


def _proj_sample_kernel(x_ref, tab_ref, wkv_ref, wlat_ref, wgate_ref, bgate_ref, qn_ref, kvn_ref,
                        wq_ref, wuq_ref, wuk_ref,
                        km_ref, vm_ref, ckv_ref, kpe_ref, gates_ref, qm_ref, qlat_ref, qpe_ref):
    xb, _, _, _, cqn = _proj_common(x_ref, tab_ref, wkv_ref, wlat_ref, wgate_ref, bgate_ref, qn_ref,
                                    kvn_ref, km_ref, vm_ref, ckv_ref, kpe_ref, gates_ref)
    q = _dot(xb, wq_ref[...])
    for h in range(MOBA_HEADS):
        qm_ref[h] = q[:, h * MOBA_HEAD_DIM:(h + 1) * MOBA_HEAD_DIM]
    qq = _dot(cqn, wuq_ref[...])
    qpe = (qq[:, NOPE_W:NOPE_W + PE_W] * tab_ref[:, :PE_W] + qq[:, NOPE_W + PE_W:] * tab_ref[:, PE_W:]) * MLA_SCALE
    for h in range(MLA_HEADS):
        qpe_ref[h] = qpe[:, h * MLA_ROPE:(h + 1) * MLA_ROPE]
        q_nope = qq[:, h * MLA_NOPE:(h + 1) * MLA_NOPE].astype(BF16)
        qlat_ref[h] = _dot(q_nope, wuk_ref[h]) * MLA_SCALE


def _project(x, tab, weights, extra_in, extra_shapes, extra_specs, body, name):
    n = x.shape[0]
    tb = MOBA_BLOCK
    row = lambda w: pl.BlockSpec((tb, w), lambda i: (i, 0))
    full = lambda a: pl.BlockSpec(a.shape, lambda i: (0,) * a.ndim)
    common_shapes = (
        jax.ShapeDtypeStruct((n, KV_W), F32), jax.ShapeDtypeStruct((n, KV_W), F32),
        jax.ShapeDtypeStruct((n, MLA_KV_LORA), F32), jax.ShapeDtypeStruct((n, MLA_ROPE), F32),
        jax.ShapeDtypeStruct((2 * D_MODEL, n), F32),
    )
    common_specs = (row(KV_W), row(KV_W), row(MLA_KV_LORA), row(MLA_ROPE),
                    pl.BlockSpec((2 * D_MODEL, tb), lambda i: (0, i)))
    operands = (x,) + tuple(tab) + tuple(weights) + tuple(extra_in)
    in_specs = [row(D_MODEL), row(tab[0].shape[1])]
    if len(tab) == 2:
        in_specs.append(pl.BlockSpec((tab[1].shape[0], tb), lambda i: (0, i)))
    in_specs += [full(a) for a in tuple(weights) + tuple(extra_in)]
    return pl.pallas_call(
        body,
        grid=(n // tb,),
        in_specs=in_specs,
        out_specs=common_specs + tuple(extra_specs),
        out_shape=common_shapes + tuple(extra_shapes),
        compiler_params=_params("parallel"),
        name=name,
    )(*operands)


def _project_prompt(x, tab, tab_t, weights, wq_t, wv_t, wuq_t, wuk_t):
    n = x.shape[0]
    tb = MOBA_BLOCK
    nb = n // tb
    shapes = (
        jax.ShapeDtypeStruct((MOBA_HEADS, MOBA_HEAD_DIM, n), F32),
        jax.ShapeDtypeStruct((MOBA_KV_HEADS, n, MOBA_HEAD_DIM), BF16),
        jax.ShapeDtypeStruct((MOBA_KV_HEADS, nb, MOBA_HEAD_DIM, tb), BF16),
        jax.ShapeDtypeStruct((nb, 1, KV_W), F32),
        jax.ShapeDtypeStruct((n, MLA_KV_LORA), BF16),
        jax.ShapeDtypeStruct((n, MLA_ROPE), BF16),
        jax.ShapeDtypeStruct((nb, MLA_KV_LORA, tb), BF16),
        jax.ShapeDtypeStruct((MLA_HEADS, MLA_KV_LORA, n), BF16),
        jax.ShapeDtypeStruct((MLA_HEADS, MLA_ROPE, n), BF16),
    )
    specs = (
        pl.BlockSpec((MOBA_HEADS, MOBA_HEAD_DIM, tb), lambda i: (0, 0, i)),
        pl.BlockSpec((MOBA_KV_HEADS, tb, MOBA_HEAD_DIM), lambda i: (0, i, 0)),
        pl.BlockSpec((MOBA_KV_HEADS, 1, MOBA_HEAD_DIM, tb), lambda i: (0, i, 0, 0)),
        pl.BlockSpec((1, 1, KV_W), lambda i: (i, 0, 0)),
        pl.BlockSpec((tb, MLA_KV_LORA), lambda i: (i, 0)),
        pl.BlockSpec((tb, MLA_ROPE), lambda i: (i, 0)),
        pl.BlockSpec((1, MLA_KV_LORA, tb), lambda i: (i, 0, 0)),
        pl.BlockSpec((MLA_HEADS, MLA_KV_LORA, tb), lambda i: (0, 0, i)),
        pl.BlockSpec((MLA_HEADS, MLA_ROPE, tb), lambda i: (0, 0, i)),
    )
    return _project(x, (tab, tab_t), weights, (wq_t, wv_t, wuq_t, wuk_t), shapes, specs,
                    _proj_prompt_kernel, "prompt_projection")


def _project_sample(x, tab, weights, wq, wuq, wuk):
    n = x.shape[0]
    tb = MOBA_BLOCK
    shapes = (
        jax.ShapeDtypeStruct((MOBA_HEADS, n, MOBA_HEAD_DIM), F32),
        jax.ShapeDtypeStruct((MLA_HEADS, n, MLA_KV_LORA), F32),
        jax.ShapeDtypeStruct((MLA_HEADS, n, MLA_ROPE), F32),
    )
    head = lambda nh, w: pl.BlockSpec((nh, tb, w), lambda i: (0, i, 0))
    specs = (head(MOBA_HEADS, MOBA_HEAD_DIM), head(MLA_HEADS, MLA_KV_LORA), head(MLA_HEADS, MLA_ROPE))
    return _project(x, (tab,), weights, (wq, wuq, wuk), shapes, specs, _proj_sample_kernel, "sample_projection")


def _prompt_mla_kernel(qlat_ref, qpe_ref, ckv_ref, kpe_ref, ckvt_ref, out_ref, wq_ref, wpe_ref,
                       m_ref, l_ref, acc_ref, *, tq, tk):
    c = pl.program_id(1)
    pair_cols = 2 * tq
    n_pairs = MLA_HEADS // 2
    for p in range(n_pairs):
        wq_ref[p] = jnp.concatenate([qlat_ref[2 * p], qlat_ref[2 * p + 1]], axis=1)
        wpe_ref[p] = jnp.concatenate([qpe_ref[2 * p], qpe_ref[2 * p + 1]], axis=1)
    _softmax_init(m_ref, l_ref, acc_ref)

    def step(blk0, n_keys, masked):
        n_sub = n_keys // MOBA_BLOCK
        start = pl.multiple_of(blk0 * MOBA_BLOCK, MOBA_BLOCK)
        k1 = ckv_ref[pl.ds(start, n_keys), :]
        k2 = kpe_ref[pl.ds(start, n_keys), :]
        v_t = [ckvt_ref[blk0 + u] for u in range(n_sub)]
        scores = []
        for pr in range(n_pairs):
            s = _dot(k1, wq_ref[pr]) + _dot(k2, wpe_ref[pr])
            if masked:
                q_pos = c * tq + jnp.bitwise_and(lax.broadcasted_iota(jnp.int32, s.shape, 1), tq - 1)
                k_pos = start + lax.broadcasted_iota(jnp.int32, s.shape, 0)
                s = jnp.where(k_pos <= q_pos, s, NEG_INF)
            scores.append(s)
        m_old = m_ref[...]
        l_old = l_ref[...]
        m_new = jnp.concatenate(
            [jnp.maximum(m_old[:, pr * pair_cols:(pr + 1) * pair_cols], jnp.max(scores[pr], axis=0, keepdims=True))
             for pr in range(n_pairs)], axis=1)
        alpha = jnp.exp(m_old - m_new)
        probs = [jnp.exp(scores[pr] - m_new[:, pr * pair_cols:(pr + 1) * pair_cols]) for pr in range(n_pairs)]
        l_ref[...] = alpha * l_old + jnp.concatenate([jnp.sum(p, axis=0, keepdims=True) for p in probs], axis=1)
        m_ref[...] = m_new
        for pr in range(n_pairs):
            cols = slice(pr * pair_cols, (pr + 1) * pair_cols)
            p = probs[pr].astype(BF16)
            pv = None
            for u in range(n_sub):
                t = _dot(v_t[u], p[u * MOBA_BLOCK:(u + 1) * MOBA_BLOCK])
                pv = t if pv is None else pv + t
            acc_ref[:, cols] = alpha[:, cols] * acc_ref[:, cols] + pv

    short = tk // 2
    n_long = (c * tq) // tk
    blocks_per_long = tk // MOBA_BLOCK
    blocks_per_short = short // MOBA_BLOCK

    def body(j, carry):
        step(j * blocks_per_long, tk, False)
        return carry

    lax.fori_loop(0, n_long, body, 0)
    n_short = (c * tq) // short

    @pl.when(n_short > 2 * n_long)
    def _():
        step(2 * n_long * blocks_per_short, short, False)

    step(n_short * blocks_per_short, short, True)
    for h in range(MLA_HEADS):
        cols = slice(h * tq, (h + 1) * tq)
        out_ref[h] = acc_ref[:, cols] / l_ref[:, cols]


def _prompt_mla(qlat_t, qpe_t, ckvb, kpeb, ckv_t, batch, seq):
    tq = Q_BLOCK
    tk = min(1024, seq)
    nq = seq // tq
    cols = MLA_HEADS * tq
    n = batch * seq
    return pl.pallas_call(
        functools.partial(_prompt_mla_kernel, tq=tq, tk=tk),
        grid=(batch, nq),
        in_specs=[
            pl.BlockSpec((MLA_HEADS, MLA_KV_LORA, tq), lambda b, c: (0, 0, b * nq + c)),
            pl.BlockSpec((MLA_HEADS, MLA_ROPE, tq), lambda b, c: (0, 0, b * nq + c)),
            pl.BlockSpec((seq, MLA_KV_LORA), lambda b, c: (b, 0)),
            pl.BlockSpec((seq, MLA_ROPE), lambda b, c: (b, 0)),
            pl.BlockSpec((seq // MOBA_BLOCK, MLA_KV_LORA, MOBA_BLOCK), lambda b, c: (b, 0, 0)),
        ],
        out_specs=pl.BlockSpec((MLA_HEADS, MLA_KV_LORA, tq), lambda b, c: (0, 0, b * nq + c)),
        out_shape=jax.ShapeDtypeStruct((MLA_HEADS, MLA_KV_LORA, n), F32),
        scratch_shapes=[pltpu.VMEM((MLA_HEADS // 2, MLA_KV_LORA, 2 * tq), BF16),
                        pltpu.VMEM((MLA_HEADS // 2, MLA_ROPE, 2 * tq), BF16),
                        pltpu.VMEM((1, cols), F32), pltpu.VMEM((1, cols), F32),
                        pltpu.VMEM((MLA_KV_LORA, cols), F32)],
        compiler_params=_params("parallel", "arbitrary"),
        name="prompt_mla",
    )(qlat_t, qpe_t, ckvb, kpeb, ckv_t)


def _prompt_moba_kernel(qt_ref, kb_ref, vt_ref, kmean_ref, bown_ref, bprev_ref, bfar_ref, out_ref,
                        sel_ref, m_ref, l_ref, acc_ref, *, tq, n_blk):
    c = pl.program_id(2)
    cols = MOBA_GROUP * tq
    own = (c * tq) // MOBA_BLOCK
    q_t = jnp.concatenate([qt_ref[u] for u in range(MOBA_GROUP)], axis=1)
    qb_t = (q_t * MOBA_SCALE).astype(BF16)

    s_blk = _dot(kmean_ref[0], q_t, precision=HIGHEST)
    blk = lax.broadcasted_iota(jnp.int32, (n_blk, cols), 0)
    s_blk = jnp.where(blk < own, s_blk, NEG_INF)
    sel_ref[...] = jnp.where(blk < own, _top_mask(s_blk, MOBA_TOPK, 0), 0.0)

    _softmax_init(m_ref, l_ref, acc_ref)

    last_group = own // BLOCKS_PER_STEP

    def group_step(jj, near):
        first = jj * BLOCKS_PER_STEP
        start = pl.multiple_of(first * MOBA_BLOCK, BLOCKS_PER_STEP * MOBA_BLOCK)
        s_all = _dot(kb_ref[0, pl.ds(start, BLOCKS_PER_STEP * MOBA_BLOCK), :], qb_t)
        parts = []
        for u in range(BLOCKS_PER_STEP):
            j = first + u
            s = s_all[u * MOBA_BLOCK:(u + 1) * MOBA_BLOCK]
            picked = sel_ref[pl.ds(j, 1), :]
            if near:
                bias = jnp.where(j == own, bown_ref[0, 0], jnp.where(j == own - 1, bprev_ref[0, 0], bfar_ref[0]))
                picked = jnp.where(j == own, 1.0, picked)
                parts.append(jnp.where(picked > 0.5, s + bias, NEG_INF))
            else:
                parts.append(s + jnp.where(picked > 0.5, bfar_ref[0], NEG_INF))
        m_old = m_ref[...]
        m_new = m_old
        for s in parts:
            m_new = jnp.maximum(m_new, jnp.max(s, axis=0, keepdims=True))
        alpha = jnp.exp(m_old - m_new)
        l_new = alpha * l_ref[...]
        pv = None
        for u in range(BLOCKS_PER_STEP):
            p = jnp.exp(parts[u] - m_new)
            l_new = l_new + jnp.sum(p, axis=0, keepdims=True)
            t = _dot(vt_ref[0, first + u], p.astype(BF16))
            pv = t if pv is None else pv + t
        l_ref[...] = l_new
        m_ref[...] = m_new
        acc_ref[...] = alpha * acc_ref[...] + pv

    def far(jj, carry):
        group_step(jj, False)
        return carry

    lax.fori_loop(0, last_group - 1, far, 0)

    @pl.when(last_group >= 1)
    def _():
        group_step(last_group - 1, True)

    group_step(last_group, True)
    out = acc_ref[...] / l_ref[...]
    for u in range(MOBA_GROUP):
        out_ref[u] = out[:, u * tq:(u + 1) * tq]


def _prompt_moba(qm_t, kb, vb_t, kmean_g, bias_own, bias_prev, bias_far, batch, seq):
    tq = Q_BLOCK
    nq = seq // tq
    n_blk = seq // MOBA_BLOCK
    cols = MOBA_GROUP * tq
    per_block = MOBA_BLOCK // tq
    n = batch * seq
    return pl.pallas_call(
        functools.partial(_prompt_moba_kernel, tq=tq, n_blk=n_blk),
        grid=(batch, MOBA_KV_HEADS, nq),
        in_specs=[
            pl.BlockSpec((MOBA_GROUP, MOBA_HEAD_DIM, tq), lambda b, g, c: (g, 0, b * nq + c)),
            pl.BlockSpec((1, seq, MOBA_HEAD_DIM), lambda b, g, c: (g, b, 0)),
            pl.BlockSpec((1, n_blk, MOBA_HEAD_DIM, MOBA_BLOCK), lambda b, g, c: (g, b, 0, 0)),
            pl.BlockSpec((1, n_blk, MOBA_HEAD_DIM), lambda b, g, c: (g, b, 0)),
            pl.BlockSpec((1, 1, MOBA_BLOCK, cols), lambda b, g, c: (g, c % per_block, 0, 0)),
            pl.BlockSpec((1, 1, MOBA_BLOCK, cols), lambda b, g, c: (g, c % per_block, 0, 0)),
            pl.BlockSpec((1, 1, cols), lambda b, g, c: (g, 0, 0)),
        ],
        out_specs=pl.BlockSpec((MOBA_GROUP, MOBA_HEAD_DIM, tq), lambda b, g, c: (g, 0, b * nq + c)),
        out_shape=jax.ShapeDtypeStruct((MOBA_HEADS, MOBA_HEAD_DIM, n), F32),
        scratch_shapes=[pltpu.VMEM((n_blk, cols), F32), pltpu.VMEM((1, cols), F32), pltpu.VMEM((1, cols), F32),
                        pltpu.VMEM((MOBA_HEAD_DIM, cols), F32)],
        compiler_params=_params("parallel", "parallel", "arbitrary"),
        name="prompt_moba",
    )(qm_t, kb, vb_t, kmean_g, bias_own, bias_prev, bias_far)


def _page_specs(block_shape, group, n_par=1):
    nd = len(block_shape)

    def make(u, k):
        def index_map(s, i, pt):
            return (0, pt[s * n_par + u, i * group + k]) + (0,) * (nd - 2)
        return pl.BlockSpec(block_shape, index_map)

    return [make(u, k) for u in range(n_par) for k in range(group)]


def _block_mean_kernel(pt_ref, *refs, group, n_blk):
    k_refs, out_ref = refs[:group], refs[group]
    i = pl.program_id(1)
    ppb = MOBA_BLOCK // PAGE_SIZE

    @pl.when(i == 0)
    def _():
        out_ref[...] = jnp.zeros(out_ref.shape, F32)

    col = lax.broadcasted_iota(jnp.int32, (MOBA_HEAD_DIM, n_blk), 1)
    for g in range(MOBA_KV_HEADS):
        cur = out_ref[0, g]
        for jb in range(group // ppb):
            acc = k_refs[jb * ppb][0, 0, g]
            for p in range(1, ppb):
                acc = acc + k_refs[jb * ppb + p][0, 0, g]
            mean = jnp.sum(acc, axis=1, keepdims=True) * (1.0 / MOBA_BLOCK)
            cur = jnp.where(col == i * (group // ppb) + jb, mean, cur)
        out_ref[0, g] = cur


def _block_means(page_table, cache_kt, group):
    n_seq, n_pages = page_table.shape
    n_blk = n_pages // (MOBA_BLOCK // PAGE_SIZE)
    page = (1, 1, MOBA_KV_HEADS, MOBA_HEAD_DIM, PAGE_SIZE)
    return pl.pallas_call(
        functools.partial(_block_mean_kernel, group=group, n_blk=n_blk),
        grid_spec=pltpu.PrefetchScalarGridSpec(
            num_scalar_prefetch=1,
            grid=(n_seq, n_pages // group),
            in_specs=_page_specs(page, group),
            out_specs=pl.BlockSpec((1, MOBA_KV_HEADS, MOBA_HEAD_DIM, n_blk), lambda s, i, pt: (s, 0, 0, 0)),
        ),
        out_shape=jax.ShapeDtypeStruct((n_seq, MOBA_KV_HEADS, MOBA_HEAD_DIM, n_blk), F32),
        compiler_params=_params("parallel", "arbitrary"),
        name="sample_block_means",
    )(page_table, *([cache_kt] * group))


def _sample_moba_kernel(pt_ref, *refs, group, ds, n_par):
    n_pg = n_par * group
    k_refs, v_refs = refs[:n_pg], refs[n_pg:2 * n_pg]
    (qm_ref, bm_ref, knew_ref, vnew_ref, expand_ref, blast_ref, bfar_ref, bnew_ref,
     out_ref, sel_ref, m_ref, l_ref, acc_ref) = refs[2 * n_pg:]
    i = pl.program_id(1)
    last = pl.num_programs(1) - 1
    rows = MOBA_GROUP * ds
    units = [(u, g) for u in range(n_par) for g in range(MOBA_KV_HEADS)]

    def q_of(u, g):
        q = qm_ref[g * MOBA_GROUP:(g + 1) * MOBA_GROUP, u * ds:(u + 1) * ds, :]
        return q.reshape(rows, MOBA_HEAD_DIM)

    def pages_t(page_refs, u, g):
        pages = [page_refs[u * group + k][0, 0, g] for k in range(group)]
        return jnp.concatenate(pages, axis=1).astype(BF16)

    def unit_rows(x, idx):
        return x[idx * rows:(idx + 1) * rows]

    @pl.when(i == 0)
    def _():
        _softmax_init(m_ref, l_ref, acc_ref)
        for idx, (u, g) in enumerate(units):
            s_blk = _dot(q_of(u, g), bm_ref[u, g], precision=HIGHEST)
            sel_ref[idx * rows:(idx + 1) * rows] = _top_mask(s_blk, MOBA_TOPK, 1)

    qb = [(q_of(u, g) * MOBA_SCALE).astype(BF16) for u, g in units]
    picked = _dot(sel_ref[...].astype(BF16), expand_ref[...])
    s = jnp.concatenate([_dot(qb[idx], pages_t(k_refs, u, g)) for idx, (u, g) in enumerate(units)], axis=0)
    bias = jnp.where(i == last, blast_ref[...], bfar_ref[...])
    s = jnp.where(picked > 0.5, s + bias, NEG_INF)

    def pv(p):
        return jnp.concatenate([_dot_nt(unit_rows(p, idx), pages_t(v_refs, u, g))
                                for idx, (u, g) in enumerate(units)], axis=0)

    _softmax_update(s, pv, m_ref, l_ref, acc_ref)

    @pl.when(i == last)
    def _():
        def new_rows(ref, u, g):
            return ref[u * ds:(u + 1) * ds, g * MOBA_HEAD_DIM:(g + 1) * MOBA_HEAD_DIM].astype(BF16)

        s_new = jnp.concatenate([_dot_nt(qb[idx], new_rows(knew_ref, u, g))
                                 for idx, (u, g) in enumerate(units)], axis=0) + bnew_ref[...]

        def pv_new(p):
            return jnp.concatenate([_dot(unit_rows(p, idx), new_rows(vnew_ref, u, g))
                                    for idx, (u, g) in enumerate(units)], axis=0)

        _softmax_update(s_new, pv_new, m_ref, l_ref, acc_ref)
        out = acc_ref[...] / l_ref[...]
        for idx, (u, g) in enumerate(units):
            out_ref[g * MOBA_GROUP:(g + 1) * MOBA_GROUP, u * ds:(u + 1) * ds, :] = (
                unit_rows(out, idx).reshape(MOBA_GROUP, ds, MOBA_HEAD_DIM))


def _sample_moba(page_table, cache_kt, cache_vt, qm, blk_mean_t, km, vm, expand, bias_last, bias_far, bias_new,
                 ds, group, n_par):
    n_seq, n_pages = page_table.shape
    n_blk = blk_mean_t.shape[3]
    all_rows = n_par * MOBA_KV_HEADS * MOBA_GROUP * ds
    keys = group * PAGE_SIZE
    page = (1, 1, MOBA_KV_HEADS, MOBA_HEAD_DIM, PAGE_SIZE)
    tok = n_par * ds
    const2 = lambda a: pl.BlockSpec(a.shape, lambda s, i, pt: (0, 0))
    return pl.pallas_call(
        functools.partial(_sample_moba_kernel, group=group, ds=ds, n_par=n_par),
        grid_spec=pltpu.PrefetchScalarGridSpec(
            num_scalar_prefetch=1,
            grid=(n_seq // n_par, n_pages // group),
            in_specs=_page_specs(page, group, n_par) + _page_specs(page, group, n_par) + [
                pl.BlockSpec((MOBA_HEADS, tok, MOBA_HEAD_DIM), lambda s, i, pt: (0, s, 0)),
                pl.BlockSpec((n_par, MOBA_KV_HEADS, MOBA_HEAD_DIM, n_blk), lambda s, i, pt: (s, 0, 0, 0)),
                pl.BlockSpec((tok, KV_W), lambda s, i, pt: (s, 0)),
                pl.BlockSpec((tok, KV_W), lambda s, i, pt: (s, 0)),
                pl.BlockSpec((n_blk, keys), lambda s, i, pt: (0, i)),
                const2(bias_last), const2(bias_far), const2(bias_new),
            ],
            out_specs=pl.BlockSpec((MOBA_HEADS, tok, MOBA_HEAD_DIM), lambda s, i, pt: (0, s, 0)),
            scratch_shapes=[pltpu.VMEM((all_rows, n_blk), F32),
                            pltpu.VMEM((all_rows, 1), F32),
                            pltpu.VMEM((all_rows, 1), F32),
                            pltpu.VMEM((all_rows, MOBA_HEAD_DIM), F32)],
        ),
        out_shape=jax.ShapeDtypeStruct((MOBA_HEADS, n_seq * ds, MOBA_HEAD_DIM), F32),
        compiler_params=_params("parallel", "arbitrary"),
        name="sample_moba",
    )(page_table, *([cache_kt] * (n_par * group)), *([cache_vt] * (n_par * group)), qm, blk_mean_t, km, vm,
      expand, bias_last, bias_far, bias_new)


def _sample_mla_kernel(pt_ref, *refs, group, ds, n_par):
    n_pg = n_par * group
    c_refs, p_refs = refs[:n_pg], refs[n_pg:2 * n_pg]
    qlat_ref, qpe_ref, cnew_ref, pnew_ref, out_ref, m_ref, l_ref, acc_ref = refs[2 * n_pg:]
    i = pl.program_id(1)
    last = pl.num_programs(1) - 1
    rows = MLA_HEADS * ds
    seqs = range(n_par)
    q1 = [qlat_ref[:, u * ds:(u + 1) * ds, :].reshape(rows, MLA_KV_LORA).astype(BF16) for u in seqs]
    q2 = [qpe_ref[:, u * ds:(u + 1) * ds, :].reshape(rows, MLA_ROPE).astype(BF16) for u in seqs]

    @pl.when(i == 0)
    def _():
        _softmax_init(m_ref, l_ref, acc_ref)

    k1 = [jnp.concatenate([c_refs[u * group + k][0, 0] for k in range(group)], axis=0).astype(BF16)
          for u in seqs]
    k2_t = [jnp.concatenate([p_refs[u * group + k][0, 0] for k in range(group)], axis=1).astype(BF16)
            for u in seqs]
    s = jnp.concatenate([_dot_nt(q1[u], k1[u]) + _dot(q2[u], k2_t[u]) for u in seqs], axis=0)
    _softmax_update(s, lambda p: jnp.concatenate([_dot(p[u * rows:(u + 1) * rows], k1[u]) for u in seqs], axis=0),
                    m_ref, l_ref, acc_ref)

    @pl.when(i == last)
    def _():
        c_new = [cnew_ref[u * ds:(u + 1) * ds].astype(BF16) for u in seqs]
        p_new = [pnew_ref[u * ds:(u + 1) * ds].astype(BF16) for u in seqs]
        s_new = jnp.concatenate([_dot_nt(q1[u], c_new[u]) + _dot_nt(q2[u], p_new[u]) for u in seqs], axis=0)
        tq = jnp.bitwise_and(lax.broadcasted_iota(jnp.int32, s_new.shape, 0), ds - 1)
        tk = lax.broadcasted_iota(jnp.int32, s_new.shape, 1)
        _softmax_update(jnp.where(tk <= tq, s_new, NEG_INF),
                        lambda p: jnp.concatenate([_dot(p[u * rows:(u + 1) * rows], c_new[u]) for u in seqs], axis=0),
                        m_ref, l_ref, acc_ref)
        out = acc_ref[...] / l_ref[...]
        for u in seqs:
            out_ref[:, u * ds:(u + 1) * ds, :] = out[u * rows:(u + 1) * rows].reshape(MLA_HEADS, ds, MLA_KV_LORA)


def _sample_mla(page_table, cache_ckv, cache_kpe_t, qlat, qpe, ckv, kpe, ds, group, n_par):
    n_seq, n_pages = page_table.shape
    all_rows = n_par * MLA_HEADS * ds
    tok = n_par * ds
    return pl.pallas_call(
        functools.partial(_sample_mla_kernel, group=group, ds=ds, n_par=n_par),
        grid_spec=pltpu.PrefetchScalarGridSpec(
            num_scalar_prefetch=1,
            grid=(n_seq // n_par, n_pages // group),
            in_specs=_page_specs((1, 1, PAGE_SIZE, MLA_KV_LORA), group, n_par)
            + _page_specs((1, 1, MLA_ROPE, PAGE_SIZE), group, n_par) + [
                pl.BlockSpec((MLA_HEADS, tok, MLA_KV_LORA), lambda s, i, pt: (0, s, 0)),
                pl.BlockSpec((MLA_HEADS, tok, MLA_ROPE), lambda s, i, pt: (0, s, 0)),
                pl.BlockSpec((tok, MLA_KV_LORA), lambda s, i, pt: (s, 0)),
                pl.BlockSpec((tok, MLA_ROPE), lambda s, i, pt: (s, 0)),
            ],
            out_specs=pl.BlockSpec((MLA_HEADS, tok, MLA_KV_LORA), lambda s, i, pt: (0, s, 0)),
            scratch_shapes=[pltpu.VMEM((all_rows, 1), F32), pltpu.VMEM((all_rows, 1), F32),
                            pltpu.VMEM((all_rows, MLA_KV_LORA), F32)],
        ),
        out_shape=jax.ShapeDtypeStruct((MLA_HEADS, n_seq * ds, MLA_KV_LORA), F32),
        compiler_params=_params("parallel", "arbitrary"),
        name="sample_mla",
    )(page_table, *([cache_ckv] * (n_par * group)), *([cache_kpe_t] * (n_par * group)), qlat, qpe, ckv, kpe)


def _merge_kernel(omoba_ref, olat_ref, gates_ref, x_ref, wuv_ref, wbm_ref, wbl_ref, wout_ref, g_ref, b_ref,
                  h_ref, hb_ref, *, alpha):
    y_moba = None
    y_mla = None
    for h in range(MOBA_HEADS):
        t = _dot(wbm_ref[h], omoba_ref[h].astype(BF16))
        y_moba = t if y_moba is None else y_moba + t
    for h in range(MLA_HEADS):
        o = _dot(wuv_ref[h], olat_ref[h].astype(BF16)).astype(BF16)
        t = _dot(wbl_ref[h], o)
        y_mla = t if y_mla is None else y_mla + t
    merged = gates_ref[:D_MODEL] * y_moba + gates_ref[D_MODEL:] * y_mla
    mix = _dot(wout_ref[...], merged.astype(BF16))
    h = _layer_norm(alpha * x_ref[...] + mix.T, g_ref[...], b_ref[...])
    h_ref[...] = h
    hb_ref[...] = h.astype(BF16)


def _merge(omoba_t, olat_t, gates_t, x, wuv_t, wbm_t, wbl_t, wout_t, ln_g, ln_b, alpha, name):
    n = x.shape[0]
    tb = 256
    row = lambda w: pl.BlockSpec((tb, w), lambda i: (i, 0))
    full = lambda a: pl.BlockSpec(a.shape, lambda i: (0,) * a.ndim)
    return pl.pallas_call(
        functools.partial(_merge_kernel, alpha=alpha),
        grid=(n // tb,),
        in_specs=[pl.BlockSpec((MOBA_HEADS, MOBA_HEAD_DIM, tb), lambda i: (0, 0, i)),
                  pl.BlockSpec((MLA_HEADS, MLA_KV_LORA, tb), lambda i: (0, 0, i)),
                  pl.BlockSpec((2 * D_MODEL, tb), lambda i: (0, i)),
                  row(D_MODEL), full(wuv_t), full(wbm_t), full(wbl_t), full(wout_t), full(ln_g), full(ln_b)],
        out_specs=(row(D_MODEL), row(D_MODEL)),
        out_shape=(jax.ShapeDtypeStruct((n, D_MODEL), F32), jax.ShapeDtypeStruct((n, D_MODEL), BF16)),
        compiler_params=_params("parallel"),
        name=name,
    )(omoba_t, olat_t, gates_t, x, wuv_t, wbm_t, wbl_t, wout_t, ln_g, ln_b)


def _peer_pairs():
    return [(a, b) for a in range(PEER_TOPK) for b in range(PEER_TOPK) if (a + 1) * (b + 1) <= PEER_TOPK]


def _top_values(s, n):
    vals = []
    for _ in range(n):
        m = jnp.max(s, axis=0, keepdims=True)
        vals.append(m)
        s = jnp.where(s == m, KNOCKED_OUT, s)
    return vals


def _peer_select_kernel(hb_ref, wq_ref, k1_ref, k2_ref, a_ref, n_ref, b_ref, r_ref):
    q = _dot(hb_ref[...], wq_ref[...])
    half = PEER_KEY_DIM // 2
    pairs = _peer_pairs()
    for h in range(PEER_HEADS):
        q1 = q[:, h * PEER_KEY_DIM:h * PEER_KEY_DIM + half]
        q2 = q[:, h * PEER_KEY_DIM + half:(h + 1) * PEER_KEY_DIM]
        s1 = _dot_nt(k1_ref[h], q1, precision=HIGHEST)
        s2 = _dot_nt(k2_ref[h], q2, precision=HIGHEST)
        v1 = _top_values(s1, PEER_TOPK)
        v2 = _top_values(s2, PEER_TOPK)
        cand = jnp.concatenate([v1[a] + v2[b] for a, b in pairs], axis=0)
        thr = _top_values(cand, PEER_TOPK)[-1]
        top = v1[0] + v2[0]
        z = jnp.sum(jnp.where(cand >= thr, jnp.exp(cand - top), 0.0), axis=0, keepdims=True)
        count = jnp.zeros(s1.shape, F32)
        for a in range(PEER_TOPK):
            n_a = None
            for b in range(PEER_TOPK // (a + 1)):
                t = jnp.where(v1[a] + v2[b] >= thr, 1.0, 0.0)
                n_a = t if n_a is None else n_a + t
            count = jnp.where(s1 == v1[a], n_a, count)
        rank = jnp.full(s2.shape, float(PEER_TOPK), F32)
        for k in range(PEER_TOPK):
            rank = jnp.where(s2 == v2[k], float(k), rank)
        a_ref[h] = jnp.exp(s1 - v1[0]) / z
        n_ref[h] = count
        b_ref[h] = jnp.exp(s2 - v2[0]).astype(BF16)
        r_ref[h] = rank.astype(BF16)


def _peer_select(hb, wq, k1, k2, name):
    n = hb.shape[0]
    tb = 256
    full = lambda a: pl.BlockSpec(a.shape, lambda i: (0,) * a.ndim)
    sel = pl.BlockSpec((PEER_HEADS, PEER_NKEYS, tb), lambda i: (0, 0, i))
    per_first_key = jax.ShapeDtypeStruct((PEER_HEADS, PEER_NKEYS, n), F32)
    per_second_key = jax.ShapeDtypeStruct((PEER_HEADS, PEER_NKEYS, n), BF16)
    return pl.pallas_call(
        _peer_select_kernel,
        grid=(n // tb,),
        in_specs=[pl.BlockSpec((tb, D_MODEL), lambda i: (i, 0)), full(wq), full(k1), full(k2)],
        out_specs=(sel, sel, sel, sel),
        out_shape=(per_first_key, per_first_key, per_second_key, per_second_key),
        compiler_params=_params("parallel"),
        name=name,
    )(hb, wq, k1, k2)


def _peer_expert_kernel(hb_ref, h_ref, u_ref, vt_ref, a_ref, n_ref, b_ref, r_ref, g_ref, bias_ref,
                        out_ref, acc_ref, gw_ref, *, alpha, rows_per_step):
    e = pl.program_id(1)

    @pl.when(e == 0)
    def _():
        acc_ref[...] = jnp.zeros(acc_ref.shape, F32)

    sub = PEER_SUB_ROWS * PEER_NKEYS
    n_sub = rows_per_step // PEER_SUB_ROWS
    acts = [_dot_nt(u_ref[sb * sub:(sb + 1) * sub, :], hb_ref[...]) for sb in range(n_sub)]
    partial = None
    sub_rows = range(PEER_SUB_ROWS)
    heads = range(PEER_HEADS)
    for sb in range(n_sub):
        first = e * rows_per_step + sb * PEER_SUB_ROWS
        a_full = [[a_ref[h, pl.ds(first + rr, 1), :] for h in heads] for rr in sub_rows]
        n_full = [[n_ref[h, pl.ds(first + rr, 1), :] for h in heads] for rr in sub_rows]
        tile = (BF16_SUBLANES, LANES)
        for lc in range(hb_ref.shape[0] // LANES):
            lanes = slice(lc * LANES, (lc + 1) * LANES)
            a_rows = [[jnp.broadcast_to(a_full[rr][h][:, lanes], tile).astype(BF16) for h in heads] for rr in sub_rows]
            n_rows = [[jnp.broadcast_to(n_full[rr][h][:, lanes], tile).astype(BF16) for h in heads] for rr in sub_rows]
            for sc in range(PEER_NKEYS // BF16_SUBLANES):
                keys = slice(sc * BF16_SUBLANES, (sc + 1) * BF16_SUBLANES)
                w = [None for _ in sub_rows]
                for h in heads:
                    rank_t = r_ref[h, keys, lanes]
                    b_t = b_ref[h, keys, lanes]
                    for rr in sub_rows:
                        t = jnp.where(rank_t < n_rows[rr][h], a_rows[rr][h] * b_t, jnp.zeros(tile, BF16))
                        w[rr] = t if w[rr] is None else w[rr] + t
                for rr in sub_rows:
                    rows = slice((sb * PEER_SUB_ROWS + rr) * PEER_NKEYS + sc * BF16_SUBLANES,
                                 (sb * PEER_SUB_ROWS + rr) * PEER_NKEYS + (sc + 1) * BF16_SUBLANES)
                    x = acts[sb][rr * PEER_NKEYS + sc * BF16_SUBLANES:rr * PEER_NKEYS + (sc + 1) * BF16_SUBLANES, lanes]
                    gelu = 0.5 * x * (1.0 + lax.erf(x * (2.0 ** -0.5)))
                    gw_ref[rows, lanes] = w[rr] * gelu.astype(BF16)
        t = _dot(vt_ref[:, sb * sub:(sb + 1) * sub], gw_ref[sb * sub:(sb + 1) * sub, :])
        partial = t if partial is None else partial + t
    acc_ref[...] += partial

    @pl.when(e == pl.num_programs(1) - 1)
    def _():
        f = acc_ref[...].T
        out_ref[...] = _layer_norm(alpha * h_ref[...] + f, g_ref[...], bias_ref[...])


def _peer_expert(hb, h, u_b, vt_b, sel, ln_g, ln_b, alpha, name):
    n = hb.shape[0]
    tb = 512
    rows_per_step = PEER_ROWS_PER_STEP
    n_exp = u_b.shape[0]
    et = rows_per_step * PEER_NKEYS
    sel_spec = pl.BlockSpec((PEER_HEADS, PEER_NKEYS, tb), lambda i, e: (0, 0, i))
    vec = pl.BlockSpec((1, D_MODEL), lambda i, e: (0, 0))
    return pl.pallas_call(
        functools.partial(_peer_expert_kernel, alpha=alpha, rows_per_step=rows_per_step),
        grid=(n // tb, n_exp // et),
        in_specs=[pl.BlockSpec((tb, D_MODEL), lambda i, e: (i, 0)),
                  pl.BlockSpec((tb, D_MODEL), lambda i, e: (i, 0)),
                  pl.BlockSpec((et, D_MODEL), lambda i, e: (e, 0)),
                  pl.BlockSpec((D_MODEL, et), lambda i, e: (0, e)),
                  sel_spec, sel_spec, sel_spec, sel_spec, vec, vec],
        out_specs=pl.BlockSpec((tb, D_MODEL), lambda i, e: (i, 0)),
        out_shape=jax.ShapeDtypeStruct((n, D_MODEL), F32),
        scratch_shapes=[pltpu.VMEM((D_MODEL, tb), F32), pltpu.VMEM((et, tb), BF16)],
        compiler_params=_params("parallel", "arbitrary"),
        name=name,
    )(hb, h, u_b, vt_b, *sel, ln_g, ln_b)


def _rel_bucket_static(dist):
    n = np.maximum(dist, 0)
    max_exact = REL_BUCKETS // 2
    nf = np.maximum(n, 1).astype(np.float32)
    large = max_exact + (np.log(nf / max_exact) / math.log(REL_MAX_DIST / max_exact)
                         * (REL_BUCKETS - max_exact)).astype(np.int32)
    return np.where(n < max_exact, n, np.minimum(large, REL_BUCKETS - 1)).astype(np.int32)


def _bias_tile(rel_bias, dist, causal, queries_on_lanes):
    lq, lk = dist.shape
    onehot = jax.nn.one_hot(jnp.asarray(_rel_bucket_static(dist).reshape(-1)), REL_BUCKETS, dtype=F32)
    tile = _dot(onehot, rel_bias.astype(F32), precision=HIGHEST).reshape(lq, lk, MOBA_HEADS)
    if causal:
        tile = jnp.where(jnp.asarray(dist >= 0)[:, :, None], tile, NEG_INF)
    tile = jnp.transpose(tile, (2, 0, 1)).reshape(MOBA_KV_HEADS, MOBA_GROUP * lq, lk)
    return jnp.transpose(tile, (0, 2, 1)) if queries_on_lanes else tile


def _rope_table(pos):
    half = MLA_ROPE // 2
    inv_freq = ROPE_THETA ** (-jnp.arange(half, dtype=F32) / half)
    ang = pos.astype(F32)[:, None] * inv_freq[None, :]
    cos, sin = jnp.cos(ang), jnp.sin(ang)
    cos_full = jnp.tile(jnp.concatenate([cos, cos], -1), (1, MLA_HEADS))
    sin_signed = jnp.tile(jnp.concatenate([-sin, sin], -1), (1, MLA_HEADS))
    return jnp.concatenate([cos_full, sin_signed], -1)


def _swap_halves(w):
    half = w.shape[-1] // 2
    return jnp.concatenate([w[..., half:], w[..., :half]], -1)


def _largest_divisor(n, cap):
    d = min(n, cap)
    while n % d:
        d -= 1
    return d


def kernel(x_prompt, x_sample, cache_moba_k, cache_moba_v, cache_mla_ckv, cache_mla_kpe, page_table, w_in, b_gate, mla_q_norm, mla_w_uq, mla_kv_norm, mla_w_uk, mla_w_uv, rel_bias, w_br_moba, w_br_mla, w_out, ln1_g, ln1_b, peer_wq, peer_k1, peer_k2, peer_u, peer_v, ln2_g, ln2_b):
    depth = w_in.shape[0]
    assert depth == 1, "single-layer step"
    batch, seq, _ = x_prompt.shape
    n_seq, ds, _ = x_sample.shape
    n_pages = page_table.shape[1]
    past = n_pages * PAGE_SIZE
    n_prompt = batch * seq
    n_sample = n_seq * ds
    assert seq % 512 == 0 and seq // MOBA_BLOCK >= MOBA_TOPK
    assert past % MOBA_BLOCK == 0 and past // MOBA_BLOCK >= MOBA_TOPK
    assert n_prompt % 512 == 0 and n_sample % 512 == 0 and ds == 8
    assert (seq // MOBA_BLOCK) % BLOCKS_PER_STEP == 0 and n_seq % SEQS_PER_STEP == 0
    alpha = (2 * depth) ** 0.25
    layer = 0

    w = w_in[layer]
    c0 = MOBA_W + 2 * KV_W
    lat_w = MLA_Q_LORA + MLA_KV_LORA
    wq = w[:, :MOBA_W].astype(BF16)
    wkv = w[:, MOBA_W:c0].astype(BF16)
    wv_t = w[:, MOBA_W + KV_W:c0].astype(BF16).T
    kr_w = w[:, c0 + lat_w:c0 + lat_w + MLA_ROPE]
    pad = jnp.zeros((D_MODEL, 128 - 2 * MLA_ROPE), w.dtype)
    wlat = jnp.concatenate([w[:, c0:c0 + lat_w], kr_w, _swap_halves(kr_w), pad], -1).astype(BF16)
    wgate_t = w[:, c0 + lat_w + MLA_ROPE:].astype(BF16).T
    bgate_t = b_gate[layer].reshape(-1, 1)
    uq = mla_w_uq[layer]
    uq_pe = uq[..., MLA_NOPE:]
    wuq = jnp.concatenate([uq[..., :MLA_NOPE].reshape(MLA_Q_LORA, -1), uq_pe.reshape(MLA_Q_LORA, -1),
                           _swap_halves(uq_pe).reshape(MLA_Q_LORA, -1)], -1).astype(BF16)
    wuk = jnp.transpose(mla_w_uk[layer], (1, 2, 0)).astype(BF16)
    wuk_t = jnp.transpose(mla_w_uk[layer], (1, 0, 2)).astype(BF16)
    wuv_t = jnp.transpose(mla_w_uv[layer], (1, 2, 0)).astype(BF16)
    wbm_t = jnp.transpose(w_br_moba[layer].reshape(MOBA_HEADS, MOBA_HEAD_DIM, D_MODEL), (0, 2, 1)).astype(BF16)
    wbl_t = jnp.transpose(w_br_mla[layer].reshape(MLA_HEADS, MLA_V, D_MODEL), (0, 2, 1)).astype(BF16)
    wout_t = w_out[layer].astype(BF16).T
    peer_q = peer_wq[layer].astype(BF16)
    u_b = peer_u[layer].astype(BF16)
    vt_b = peer_v[layer].astype(BF16).T
    vec = lambda a: a[layer].reshape(1, -1)
    shared = (wkv, wlat, wgate_t, bgate_t, vec(mla_q_norm), vec(mla_kv_norm))

    tab_p = _rope_table(jnp.tile(jnp.arange(seq), batch))
    tab_s = _rope_table(jnp.tile(past + jnp.arange(ds), n_seq))
    qi = np.arange(Q_BLOCK)[:, None]
    ki = np.arange(MOBA_BLOCK)[None, :]
    per_block = MOBA_BLOCK // Q_BLOCK
    bias_own = jnp.stack([_bias_tile(rel_bias, o * Q_BLOCK + qi - ki, True, True) for o in range(per_block)], 1)
    bias_prev = jnp.stack([_bias_tile(rel_bias, MOBA_BLOCK + o * Q_BLOCK + qi - ki, False, True)
                           for o in range(per_block)], 1)
    far = rel_bias[REL_BUCKETS - 1].astype(F32)
    bias_far = jnp.repeat(far, Q_BLOCK).reshape(MOBA_KV_HEADS, 1, -1)
    ti = np.arange(ds)[:, None]
    ppb = MOBA_BLOCK // PAGE_SIZE
    mean_group = _largest_divisor(n_pages // ppb, MEAN_PAGES_PER_STEP // ppb) * ppb
    group = _largest_divisor(n_pages // ppb, ATT_PAGES_PER_STEP // ppb) * ppb
    n_par = SEQS_PER_STEP
    keys = group * PAGE_SIZE
    per_seq = lambda a: jnp.tile(a.reshape(MOBA_HEADS * ds, -1), (n_par, 1))
    far_s = jnp.repeat(far, ds).reshape(MOBA_KV_HEADS, -1, 1)
    tail = _bias_tile(rel_bias, past + ti - (past - MOBA_BLOCK + ki), False, False)
    bias_last = per_seq(jnp.concatenate(
        [jnp.broadcast_to(far_s, far_s.shape[:2] + (keys - MOBA_BLOCK,)), tail], -1))
    bias_new = per_seq(_bias_tile(rel_bias, ti - np.arange(ds)[None, :], True, False))
    far_s = per_seq(far_s)
    n_blk_s = past // MOBA_BLOCK
    expand = jnp.asarray(np.repeat(np.eye(n_blk_s, dtype=np.float32), MOBA_BLOCK, axis=1), BF16)

    cache_kt = jnp.swapaxes(cache_moba_k, 3, 4)
    cache_vt = jnp.swapaxes(cache_moba_v, 3, 4)
    cache_kpe_t = jnp.swapaxes(cache_mla_kpe, 2, 3)

    xp = x_prompt.reshape(n_prompt, D_MODEL)
    (km_p, vm_p, ckv_p, kpe_p, gates_p, qm_t, kb, vb_t, kmean, ckvb, kpeb, ckv_t, qlat_t, qpe_t) = _project_prompt(
        xp, tab_p, tab_p.T, shared, wq.T, wv_t, wuq.T, wuk_t)
    kmean_g = jnp.transpose(kmean.reshape(-1, MOBA_KV_HEADS, MOBA_HEAD_DIM), (1, 0, 2))
    olat_p = _prompt_mla(qlat_t, qpe_t, ckvb, kpeb, ckv_t, batch, seq)
    omoba_p = _prompt_moba(qm_t, kb, vb_t, kmean_g, bias_own, bias_prev, bias_far, batch, seq)
    merge_w = (wuv_t, wbm_t, wbl_t, wout_t, vec(ln1_g), vec(ln1_b))
    h_p, hb_p = _merge(omoba_p, olat_p, gates_p, xp, *merge_w, alpha, "prompt_merge_ln1")
    sel_p = _peer_select(hb_p, peer_q, peer_k1[layer], peer_k2[layer], "prompt_peer_select")
    y_p = _peer_expert(hb_p, h_p, u_b, vt_b, sel_p, vec(ln2_g), vec(ln2_b), alpha, "prompt_peer_experts")

    xs = x_sample.reshape(n_sample, D_MODEL)
    km_s, vm_s, ckv_s, kpe_s, gates_s, qm, qlat, qpe = _project_sample(xs, tab_s, shared, wq, wuq, wuk)
    blk_mean_t = _block_means(page_table, cache_kt, mean_group)
    omoba_s = _sample_moba(page_table, cache_kt, cache_vt, qm, blk_mean_t, km_s, vm_s, expand, bias_last,
                           far_s, bias_new, ds, group, n_par)
    olat_s = _sample_mla(page_table, cache_mla_ckv, cache_kpe_t, qlat, qpe, ckv_s, kpe_s, ds, group, n_par)
    h_s, hb_s = _merge(jnp.swapaxes(omoba_s, 1, 2), jnp.swapaxes(olat_s, 1, 2), gates_s, xs, *merge_w, alpha,
                       "sample_merge_ln1")
    sel_s = _peer_select(hb_s, peer_q, peer_k1[layer], peer_k2[layer], "sample_peer_select")
    y_s = _peer_expert(hb_s, h_s, u_b, vt_b, sel_s, vec(ln2_g), vec(ln2_b), alpha, "sample_peer_experts")

    lead_p, lead_s = (depth, batch, seq), (depth, n_seq, ds)
    kv_tail = (MOBA_KV_HEADS, MOBA_HEAD_DIM)
    return (
        y_p.reshape(batch, seq, D_MODEL),
        y_s.reshape(n_seq, ds, D_MODEL),
        km_p.reshape(lead_p + kv_tail), vm_p.reshape(lead_p + kv_tail),
        ckv_p.reshape(lead_p + (MLA_KV_LORA,)), kpe_p.reshape(lead_p + (MLA_ROPE,)),
        km_s.reshape(lead_s + kv_tail), vm_s.reshape(lead_s + kv_tail),
        ckv_s.reshape(lead_s + (MLA_KV_LORA,)), kpe_s.reshape(lead_s + (MLA_ROPE,)),
    )
```

```python
import functools
import math

import numpy as np
import jax
import jax.numpy as jnp
from jax import lax
from jax.experimental import pallas as pl
from jax.experimental.pallas import tpu as pltpu

D_MODEL = 1024
PAGE_SIZE = 128
MOBA_HEADS = 8
MOBA_KV_HEADS = 4
MOBA_GROUP = MOBA_HEADS // MOBA_KV_HEADS
MOBA_HEAD_DIM = 64
MOBA_BLOCK = 256
MOBA_TOPK = 3
BLOCKS_PER_STEP = 8
BLOCKS_PER_UPDATE = 1
MEAN_PAGES_PER_STEP = 32
ATT_PAGES_PER_STEP = 16
SEQS_PER_STEP = 2
MLA_HEADS = 8
MLA_NOPE = 64
MLA_ROPE = 32
MLA_V = 64
MLA_Q_LORA = 256
MLA_KV_LORA = 256
ROPE_THETA = 10000.0
REL_BUCKETS = 32
REL_MAX_DIST = 128
PEER_HEADS = 8
PEER_NKEYS = 128
PEER_KEY_DIM = 128
PEER_TOPK = 16
PEER_ROWS_PER_STEP = 4
PEER_SUB_ROWS = 2
Q_BLOCK = 128
LN_EPS = 1e-5
RMS_EPS = 1e-6
NEG_INF = -1e30
KNOCKED_OUT = -3e38

KV_W = MOBA_KV_HEADS * MOBA_HEAD_DIM
MOBA_W = MOBA_HEADS * MOBA_HEAD_DIM
NOPE_W = MLA_HEADS * MLA_NOPE
PE_W = MLA_HEADS * MLA_ROPE
MLA_SCALE = (MLA_NOPE + MLA_ROPE) ** -0.5
MOBA_SCALE = MOBA_HEAD_DIM ** -0.5

VMEM_LIMIT_BYTES = 56 * 1024 * 1024
SUBLANES, LANES = 8, 128
BF16_SUBLANES = 2 * SUBLANES

BF16 = jnp.bfloat16
F32 = jnp.float32
HIGHEST = lax.Precision.HIGHEST


def _dot(a, b, precision=None):
    return jnp.dot(a, b, precision=precision, preferred_element_type=F32)


def _dot_nt(a, b, precision=None):
    return lax.dot_general(a, b, (((1,), (1,)), ((), ())), precision=precision,
                           preferred_element_type=F32)


def _params(*sem):
    return pltpu.CompilerParams(dimension_semantics=sem, vmem_limit_bytes=VMEM_LIMIT_BYTES)


def _rms(x, g):
    return x * lax.rsqrt(jnp.mean(x * x, axis=-1, keepdims=True) + RMS_EPS) * g


def _layer_norm(x, g, b):
    mu = jnp.mean(x, axis=-1, keepdims=True)
    xc = x - mu
    var = jnp.mean(xc * xc, axis=-1, keepdims=True)
    return xc * lax.rsqrt(var + LN_EPS) * g + b


def _sigmoid(x):
    return 1.0 / (1.0 + jnp.exp(-x))


def _softmax_init(m_ref, l_ref, acc_ref):
    m_ref[...] = jnp.full(m_ref.shape, NEG_INF, F32)
    l_ref[...] = jnp.zeros(l_ref.shape, F32)
    acc_ref[...] = jnp.zeros(acc_ref.shape, F32)


def _softmax_update(s, pv, m_ref, l_ref, acc_ref):
    m_old = m_ref[...]
    m_new = jnp.maximum(m_old, jnp.max(s, axis=-1, keepdims=True))
    alpha = jnp.exp(m_old - m_new)
    p = jnp.exp(s - m_new)
    l_ref[...] = alpha * l_ref[...] + jnp.sum(p, axis=-1, keepdims=True)
    acc_ref[...] = alpha * acc_ref[...] + pv(p.astype(BF16), slice(None))
    m_ref[...] = m_new


def _top_mask(s, n_sel, axis):
    pos = lax.broadcasted_iota(jnp.int32, s.shape, axis)
    sel = jnp.zeros(s.shape, F32)
    for _ in range(n_sel):
        m = jnp.max(s, axis=axis, keepdims=True)
        idx = jnp.min(jnp.where(s == m, pos, s.shape[axis]), axis=axis, keepdims=True)
        pick = pos == idx
        sel = jnp.where(pick, 1.0, sel)
        s = jnp.where(pick, KNOCKED_OUT, s)
    return sel


def _proj_common(x_ref, tab_ref, wkv_ref, wlat_ref, wgate_ref, bgate_ref, qn_ref, kvn_ref,
                 km_ref, vm_ref, ckv_ref, kpe_ref, gates_ref):
    xb = x_ref[...].astype(BF16)
    kv = _dot(xb, wkv_ref[...])
    km, vm = kv[:, :KV_W], kv[:, KV_W:]
    km_ref[...] = km
    vm_ref[...] = vm
    lat = _dot(xb, wlat_ref[...])
    cq = lat[:, :MLA_Q_LORA]
    ckv = lat[:, MLA_Q_LORA:MLA_Q_LORA + MLA_KV_LORA]
    base = MLA_Q_LORA + MLA_KV_LORA
    kr = lat[:, base:base + MLA_ROPE]
    kr_swapped = lat[:, base + MLA_ROPE:base + 2 * MLA_ROPE]
    ckv_n = _rms(ckv, kvn_ref[...])
    ckv_ref[...] = ckv_n
    kpe = kr * tab_ref[:, :MLA_ROPE] + kr_swapped * tab_ref[:, PE_W:PE_W + MLA_ROPE]
    kpe_ref[...] = kpe
    gates_ref[...] = _sigmoid(_dot_nt(wgate_ref[...], xb) + bgate_ref[...])
    cqn = _rms(cq, qn_ref[...]).astype(BF16)
    return xb, km, ckv_n, kpe, cqn


def _proj_prompt_kernel(x_ref, tab_ref, tabt_ref, wkv_ref, wlat_ref, wgate_ref, bgate_ref, qn_ref, kvn_ref,
                        wqt_ref, wvt_ref, wuqt_ref, wukt_ref,
                        km_ref, vm_ref, ckv_ref, kpe_ref, gates_ref,
                        qmt_ref, kb_ref, vbt_ref, kmean_ref, ckvb_ref, kpeb_ref, ckvt_ref, qlatt_ref, qpet_ref):
    xb, km, ckv_n, kpe, cqn = _proj_common(x_ref, tab_ref, wkv_ref, wlat_ref, wgate_ref, bgate_ref, qn_ref,
                                           kvn_ref, km_ref, vm_ref, ckv_ref, kpe_ref, gates_ref)
    q_t = _dot_nt(wqt_ref[...], xb)
    for h in range(MOBA_HEADS):
        qmt_ref[h] = q_t[h * MOBA_HEAD_DIM:(h + 1) * MOBA_HEAD_DIM]
    v_t = _dot_nt(wvt_ref[...], xb)
    for g in range(MOBA_KV_HEADS):
        kb_ref[g] = km[:, g * MOBA_HEAD_DIM:(g + 1) * MOBA_HEAD_DIM].astype(BF16)
        vbt_ref[g, 0] = v_t[g * MOBA_HEAD_DIM:(g + 1) * MOBA_HEAD_DIM].astype(BF16)
    kmean_ref[0] = jnp.mean(km, axis=0, keepdims=True)
    ckvb_ref[...] = ckv_n.astype(BF16)
    kpeb_ref[...] = kpe.astype(BF16)
    ckvt_ref[0] = ckv_n.T.astype(BF16)
    qq_t = _dot_nt(wuqt_ref[...], cqn)
    qpe_t = (qq_t[NOPE_W:NOPE_W + PE_W] * tabt_ref[:PE_W] + qq_t[NOPE_W + PE_W:] * tabt_ref[PE_W:]) * MLA_SCALE
    for h in range(MLA_HEADS):
        qpet_ref[h] = qpe_t[h * MLA_ROPE:(h + 1) * MLA_ROPE].astype(BF16)
        q_nope_t = qq_t[h * MLA_NOPE:(h + 1) * MLA_NOPE].astype(BF16)
        qlatt_ref[h] = (_dot(wukt_ref[h], q_nope_t) * MLA_SCALE).astype(BF16)


def _proj_sample_kernel(x_ref, tab_ref, wkv_ref, wlat_ref, wgate_ref, bgate_ref, qn_ref, kvn_ref,
                        wq_ref, wuq_ref, wuk_ref,
                        km_ref, vm_ref, ckv_ref, kpe_ref, gates_ref, qm_ref, qlat_ref, qpe_ref):
    xb, _, _, _, cqn = _proj_common(x_ref, tab_ref, wkv_ref, wlat_ref, wgate_ref, bgate_ref, qn_ref,
                                    kvn_ref, km_ref, vm_ref, ckv_ref, kpe_ref, gates_ref)
    q = _dot(xb, wq_ref[...])
    for h in range(MOBA_HEADS):
        qm_ref[h] = q[:, h * MOBA_HEAD_DIM:(h + 1) * MOBA_HEAD_DIM]
    qq = _dot(cqn, wuq_ref[...])
    qpe = (qq[:, NOPE_W:NOPE_W + PE_W] * tab_ref[:, :PE_W] + qq[:, NOPE_W + PE_W:] * tab_ref[:, PE_W:]) * MLA_SCALE
    for h in range(MLA_HEADS):
        qpe_ref[h] = qpe[:, h * MLA_ROPE:(h + 1) * MLA_ROPE]
        q_nope = qq[:, h * MLA_NOPE:(h + 1) * MLA_NOPE].astype(BF16)
        qlat_ref[h] = _dot(q_nope, wuk_ref[h]) * MLA_SCALE


def _project(x, tab, weights, extra_in, extra_shapes, extra_specs, body, name):
    n = x.shape[0]
    tb = MOBA_BLOCK
    row = lambda w: pl.BlockSpec((tb, w), lambda i: (i, 0))
    full = lambda a: pl.BlockSpec(a.shape, lambda i: (0,) * a.ndim)
    common_shapes = (
        jax.ShapeDtypeStruct((n, KV_W), F32), jax.ShapeDtypeStruct((n, KV_W), F32),
        jax.ShapeDtypeStruct((n, MLA_KV_LORA), F32), jax.ShapeDtypeStruct((n, MLA_ROPE), F32),
        jax.ShapeDtypeStruct((2 * D_MODEL, n), F32),
    )
    common_specs = (row(KV_W), row(KV_W), row(MLA_KV_LORA), row(MLA_ROPE),
                    pl.BlockSpec((2 * D_MODEL, tb), lambda i: (0, i)))
    operands = (x,) + tuple(tab) + tuple(weights) + tuple(extra_in)
    in_specs = [row(D_MODEL), row(tab[0].shape[1])]
    if len(tab) == 2:
        in_specs.append(pl.BlockSpec((tab[1].shape[0], tb), lambda i: (0, i)))
    in_specs += [full(a) for a in tuple(weights) + tuple(extra_in)]
    return pl.pallas_call(
        body,
        grid=(n // tb,),
        in_specs=in_specs,
        out_specs=common_specs + tuple(extra_specs),
        out_shape=common_shapes + tuple(extra_shapes),
        compiler_params=_params("parallel"),
        name=name,
    )(*operands)


def _project_prompt(x, tab, tab_t, weights, wq_t, wv_t, wuq_t, wuk_t):
    n = x.shape[0]
    tb = MOBA_BLOCK
    nb = n // tb
    shapes = (
        jax.ShapeDtypeStruct((MOBA_HEADS, MOBA_HEAD_DIM, n), F32),
        jax.ShapeDtypeStruct((MOBA_KV_HEADS, n, MOBA_HEAD_DIM), BF16),
        jax.ShapeDtypeStruct((MOBA_KV_HEADS, nb, MOBA_HEAD_DIM, tb), BF16),
        jax.ShapeDtypeStruct((nb, 1, KV_W), F32),
        jax.ShapeDtypeStruct((n, MLA_KV_LORA), BF16),
        jax.ShapeDtypeStruct((n, MLA_ROPE), BF16),
        jax.ShapeDtypeStruct((nb, MLA_KV_LORA, tb), BF16),
        jax.ShapeDtypeStruct((MLA_HEADS, MLA_KV_LORA, n), BF16),
        jax.ShapeDtypeStruct((MLA_HEADS, MLA_ROPE, n), BF16),
    )
    specs = (
        pl.BlockSpec((MOBA_HEADS, MOBA_HEAD_DIM, tb), lambda i: (0, 0, i)),
        pl.BlockSpec((MOBA_KV_HEADS, tb, MOBA_HEAD_DIM), lambda i: (0, i, 0)),
        pl.BlockSpec((MOBA_KV_HEADS, 1, MOBA_HEAD_DIM, tb), lambda i: (0, i, 0, 0)),
        pl.BlockSpec((1, 1, KV_W), lambda i: (i, 0, 0)),
        pl.BlockSpec((tb, MLA_KV_LORA), lambda i: (i, 0)),
        pl.BlockSpec((tb, MLA_ROPE), lambda i: (i, 0)),
        pl.BlockSpec((1, MLA_KV_LORA, tb), lambda i: (i, 0, 0)),
        pl.BlockSpec((MLA_HEADS, MLA_KV_LORA, tb), lambda i: (0, 0, i)),
        pl.BlockSpec((MLA_HEADS, MLA_ROPE, tb), lambda i: (0, 0, i)),
    )
    return _project(x, (tab, tab_t), weights, (wq_t, wv_t, wuq_t, wuk_t), shapes, specs,
                    _proj_prompt_kernel, "prompt_projection")


def _project_sample(x, tab, weights, wq, wuq, wuk):
    n = x.shape[0]
    tb = MOBA_BLOCK
    shapes = (
        jax.ShapeDtypeStruct((MOBA_HEADS, n, MOBA_HEAD_DIM), F32),
        jax.ShapeDtypeStruct((MLA_HEADS, n, MLA_KV_LORA), F32),
        jax.ShapeDtypeStruct((MLA_HEADS, n, MLA_ROPE), F32),
    )
    head = lambda nh, w: pl.BlockSpec((nh, tb, w), lambda i: (0, i, 0))
    specs = (head(MOBA_HEADS, MOBA_HEAD_DIM), head(MLA_HEADS, MLA_KV_LORA), head(MLA_HEADS, MLA_ROPE))
    return _project(x, (tab,), weights, (wq, wuq, wuk), shapes, specs, _proj_sample_kernel, "sample_projection")


def _prompt_mla_kernel(qlat_ref, qpe_ref, ckv_ref, kpe_ref, ckvt_ref, out_ref, wq_ref, wpe_ref,
                       m_ref, l_ref, acc_ref, *, tq, tk):
    c = pl.program_id(1)
    pair_cols = 2 * tq
    n_pairs = MLA_HEADS // 2
    for p in range(n_pairs):
        wq_ref[p] = jnp.concatenate([qlat_ref[2 * p], qlat_ref[2 * p + 1]], axis=1)
        wpe_ref[p] = jnp.concatenate([qpe_ref[2 * p], qpe_ref[2 * p + 1]], axis=1)
    _softmax_init(m_ref, l_ref, acc_ref)

    def step(blk0, n_keys, masked):
        n_sub = n_keys // MOBA_BLOCK
        start = pl.multiple_of(blk0 * MOBA_BLOCK, MOBA_BLOCK)
        k1 = ckv_ref[pl.ds(start, n_keys), :]
        k2 = kpe_ref[pl.ds(start, n_keys), :]
        v_t = [ckvt_ref[blk0 + u] for u in range(n_sub)]
        scores = []
        for pr in range(n_pairs):
            s = _dot(k1, wq_ref[pr]) + _dot(k2, wpe_ref[pr])
            if masked:
                q_pos = c * tq + jnp.bitwise_and(lax.broadcasted_iota(jnp.int32, s.shape, 1), tq - 1)
                k_pos = start + lax.broadcasted_iota(jnp.int32, s.shape, 0)
                s = jnp.where(k_pos <= q_pos, s, NEG_INF)
            scores.append(s)
        m_old = m_ref[...]
        l_old = l_ref[...]
        m_new = jnp.concatenate(
            [jnp.maximum(m_old[:, pr * pair_cols:(pr + 1) * pair_cols], jnp.max(scores[pr], axis=0, keepdims=True))
             for pr in range(n_pairs)], axis=1)
        alpha = jnp.exp(m_old - m_new)
        probs = [jnp.exp(scores[pr] - m_new[:, pr * pair_cols:(pr + 1) * pair_cols]) for pr in range(n_pairs)]
        l_ref[...] = alpha * l_old + jnp.concatenate([jnp.sum(p, axis=0, keepdims=True) for p in probs], axis=1)
        m_ref[...] = m_new
        for pr in range(n_pairs):
            cols = slice(pr * pair_cols, (pr + 1) * pair_cols)
            p = probs[pr].astype(BF16)
            pv = None
            for u in range(n_sub):
                t = _dot(v_t[u], p[u * MOBA_BLOCK:(u + 1) * MOBA_BLOCK])
                pv = t if pv is None else pv + t
            acc_ref[:, cols] = alpha[:, cols] * acc_ref[:, cols] + pv

    short = tk // 2
    n_long = (c * tq) // tk
    blocks_per_long = tk // MOBA_BLOCK
    blocks_per_short = short // MOBA_BLOCK

    def body(j, carry):
        step(j * blocks_per_long, tk, False)
        return carry

    lax.fori_loop(0, n_long, body, 0)
    n_short = (c * tq) // short

    @pl.when(n_short > 2 * n_long)
    def _():
        step(2 * n_long * blocks_per_short, short, False)

    step(n_short * blocks_per_short, short, True)
    for h in range(MLA_HEADS):
        cols = slice(h * tq, (h + 1) * tq)
        out_ref[h] = acc_ref[:, cols] / l_ref[:, cols]


def _prompt_mla(qlat_t, qpe_t, ckvb, kpeb, ckv_t, batch, seq):
    tq = Q_BLOCK
    tk = min(1024, seq)
    nq = seq // tq
    cols = MLA_HEADS * tq
    n = batch * seq
    return pl.pallas_call(
        functools.partial(_prompt_mla_kernel, tq=tq, tk=tk),
        grid=(batch, nq),
        in_specs=[
            pl.BlockSpec((MLA_HEADS, MLA_KV_LORA, tq), lambda b, c: (0, 0, b * nq + c)),
            pl.BlockSpec((MLA_HEADS, MLA_ROPE, tq), lambda b, c: (0, 0, b * nq + c)),
            pl.BlockSpec((seq, MLA_KV_LORA), lambda b, c: (b, 0)),
            pl.BlockSpec((seq, MLA_ROPE), lambda b, c: (b, 0)),
            pl.BlockSpec((seq // MOBA_BLOCK, MLA_KV_LORA, MOBA_BLOCK), lambda b, c: (b, 0, 0)),
        ],
        out_specs=pl.BlockSpec((MLA_HEADS, MLA_KV_LORA, tq), lambda b, c: (0, 0, b * nq + c)),
        out_shape=jax.ShapeDtypeStruct((MLA_HEADS, MLA_KV_LORA, n), F32),
        scratch_shapes=[pltpu.VMEM((MLA_HEADS // 2, MLA_KV_LORA, 2 * tq), BF16),
                        pltpu.VMEM((MLA_HEADS // 2, MLA_ROPE, 2 * tq), BF16),
                        pltpu.VMEM((1, cols), F32), pltpu.VMEM((1, cols), F32),
                        pltpu.VMEM((MLA_KV_LORA, cols), F32)],
        compiler_params=_params("parallel", "arbitrary"),
        name="prompt_mla",
    )(qlat_t, qpe_t, ckvb, kpeb, ckv_t)


def _prompt_moba_kernel(qt_ref, kb_ref, vt_ref, kmean_ref, bias_ref, out_ref,
                        sel_ref, m_ref, l_ref, acc_ref, *, tq, n_blk):
    c = pl.program_id(2)
    cols = MOBA_GROUP * tq
    own = (c * tq) // MOBA_BLOCK
    q_t = jnp.concatenate([qt_ref[u] for u in range(MOBA_GROUP)], axis=1)
    qb_t = (q_t * MOBA_SCALE).astype(BF16)

    s_blk = _dot(kmean_ref[0], q_t, precision=HIGHEST)
    blk = lax.broadcasted_iota(jnp.int32, (n_blk, cols), 0)
    s_blk = jnp.where(blk < own, s_blk, NEG_INF)
    sel_ref[...] = jnp.where(blk < own, _top_mask(s_blk, MOBA_TOPK, 0), 0.0)

    _softmax_init(m_ref, l_ref, acc_ref)

    def group_step(jj, carry):
        first = jj * BLOCKS_PER_STEP
        start = pl.multiple_of(first * MOBA_BLOCK, BLOCKS_PER_STEP * MOBA_BLOCK)
        s_all = _dot(kb_ref[0, pl.ds(start, BLOCKS_PER_STEP * MOBA_BLOCK), :], qb_t)
        m, l, acc = m_ref[...], l_ref[...], acc_ref[...]
        for ss in range(BLOCKS_PER_STEP // BLOCKS_PER_UPDATE):
            parts = []
            for u in range(BLOCKS_PER_UPDATE):
                off = ss * BLOCKS_PER_UPDATE + u
                j = first + off
                bias = bias_ref[0, 0, jnp.clip(j - own + 2, 0, 2)]
                picked = jnp.where(j == own, 1.0, sel_ref[pl.ds(j, 1), :])
                s = s_all[off * MOBA_BLOCK:(off + 1) * MOBA_BLOCK]
                parts.append(jnp.where(picked > 0.5, s + bias, NEG_INF))
            m_new = m
            for s in parts:
                m_new = jnp.maximum(m_new, jnp.max(s, axis=0, keepdims=True))
            alpha = jnp.exp(m - m_new)
            l = alpha * l
            pv = None
            for u in range(BLOCKS_PER_UPDATE):
                p = jnp.exp(parts[u] - m_new)
                l = l + jnp.sum(p, axis=0, keepdims=True)
                t = _dot(vt_ref[0, first + ss * BLOCKS_PER_UPDATE + u], p.astype(BF16))
                pv = t if pv is None else pv + t
            acc = alpha * acc + pv
            m = m_new
        m_ref[...], l_ref[...], acc_ref[...] = m, l, acc
        return carry

    lax.fori_loop(0, own // BLOCKS_PER_STEP + 1, group_step, 0)
    out = acc_ref[...] / l_ref[...]
    for u in range(MOBA_GROUP):
        out_ref[u] = out[:, u * tq:(u + 1) * tq]


def _prompt_moba(qm_t, kb, vb_t, kmean_g, bias_tiles, batch, seq):
    tq = Q_BLOCK
    nq = seq // tq
    n_blk = seq // MOBA_BLOCK
    cols = MOBA_GROUP * tq
    per_block = MOBA_BLOCK // tq
    n = batch * seq
    return pl.pallas_call(
        functools.partial(_prompt_moba_kernel, tq=tq, n_blk=n_blk),
        grid=(batch, MOBA_KV_HEADS, nq),
        in_specs=[
            pl.BlockSpec((MOBA_GROUP, MOBA_HEAD_DIM, tq), lambda b, g, c: (g, 0, b * nq + c)),
            pl.BlockSpec((1, seq, MOBA_HEAD_DIM), lambda b, g, c: (g, b, 0)),
            pl.BlockSpec((1, n_blk, MOBA_HEAD_DIM, MOBA_BLOCK), lambda b, g, c: (g, b, 0, 0)),
            pl.BlockSpec((1, n_blk, MOBA_HEAD_DIM), lambda b, g, c: (g, b, 0)),
            pl.BlockSpec((1, 1, 3, MOBA_BLOCK, cols), lambda b, g, c: (g, c % per_block, 0, 0, 0)),
        ],
        out_specs=pl.BlockSpec((MOBA_GROUP, MOBA_HEAD_DIM, tq), lambda b, g, c: (g, 0, b * nq + c)),
        out_shape=jax.ShapeDtypeStruct((MOBA_HEADS, MOBA_HEAD_DIM, n), F32),
        scratch_shapes=[pltpu.VMEM((n_blk, cols), F32), pltpu.VMEM((1, cols), F32), pltpu.VMEM((1, cols), F32),
                        pltpu.VMEM((MOBA_HEAD_DIM, cols), F32)],
        compiler_params=_params("parallel", "parallel", "arbitrary"),
        name="prompt_moba",
    )(qm_t, kb, vb_t, kmean_g, bias_tiles)


def _page_specs(block_shape, group, n_par=1):
    nd = len(block_shape)

    def make(u, k):
        def index_map(s, i, pt):
            return (0, pt[s * n_par + u, i * group + k]) + (0,) * (nd - 2)
        return pl.BlockSpec(block_shape, index_map)

    return [make(u, k) for u in range(n_par) for k in range(group)]


def _block_mean_kernel(pt_ref, *refs, group, n_blk):
    k_refs, out_ref = refs[:group], refs[group]
    i = pl.program_id(1)
    ppb = MOBA_BLOCK // PAGE_SIZE

    @pl.when(i == 0)
    def _():
        out_ref[...] = jnp.zeros(out_ref.shape, F32)

    col = lax.broadcasted_iota(jnp.int32, (MOBA_HEAD_DIM, n_blk), 1)
    for g in range(MOBA_KV_HEADS):
        cur = out_ref[0, g]
        for jb in range(group // ppb):
            acc = k_refs[jb * ppb][0, 0, g]
            for p in range(1, ppb):
                acc = acc + k_refs[jb * ppb + p][0, 0, g]
            mean = jnp.sum(acc, axis=1, keepdims=True) * (1.0 / MOBA_BLOCK)
            cur = jnp.where(col == i * (group // ppb) + jb, mean, cur)
        out_ref[0, g] = cur


def _block_means(page_table, cache_kt, group):
    n_seq, n_pages = page_table.shape
    n_blk = n_pages // (MOBA_BLOCK // PAGE_SIZE)
    page = (1, 1, MOBA_KV_HEADS, MOBA_HEAD_DIM, PAGE_SIZE)
    return pl.pallas_call(
        functools.partial(_block_mean_kernel, group=group, n_blk=n_blk),
        grid_spec=pltpu.PrefetchScalarGridSpec(
            num_scalar_prefetch=1,
            grid=(n_seq, n_pages // group),
            in_specs=_page_specs(page, group),
            out_specs=pl.BlockSpec((1, MOBA_KV_HEADS, MOBA_HEAD_DIM, n_blk), lambda s, i, pt: (s, 0, 0, 0)),
        ),
        out_shape=jax.ShapeDtypeStruct((n_seq, MOBA_KV_HEADS, MOBA_HEAD_DIM, n_blk), F32),
        compiler_params=_params("parallel", "arbitrary"),
        name="sample_block_means",
    )(page_table, *([cache_kt] * group))


def _sample_moba_kernel(pt_ref, *refs, group, ds, n_par):
    n_pg = n_par * group
    k_refs, v_refs = refs[:n_pg], refs[n_pg:2 * n_pg]
    (qm_ref, bm_ref, knew_ref, vnew_ref, expand_ref, blast_ref, bfar_ref, bnew_ref,
     out_ref, sel_ref, m_ref, l_ref, acc_ref) = refs[2 * n_pg:]
    i = pl.program_id(1)
    last = pl.num_programs(1) - 1
    rows = MOBA_GROUP * ds
    units = [(u, g) for u in range(n_par) for g in range(MOBA_KV_HEADS)]

    def q_of(u, g):
        q = qm_ref[g * MOBA_GROUP:(g + 1) * MOBA_GROUP, u * ds:(u + 1) * ds, :]
        return q.reshape(rows, MOBA_HEAD_DIM)

    def pages_t(page_refs, u, g):
        pages = [page_refs[u * group + k][0, 0, g] for k in range(group)]
        return jnp.concatenate(pages, axis=1).astype(BF16)

    def unit_rows(x, idx):
        return x[idx * rows:(idx + 1) * rows]

    @pl.when(i == 0)
    def _():
        _softmax_init(m_ref, l_ref, acc_ref)
        for idx, (u, g) in enumerate(units):
            s_blk = _dot(q_of(u, g), bm_ref[u, g], precision=HIGHEST)
            sel_ref[idx * rows:(idx + 1) * rows] = _top_mask(s_blk, MOBA_TOPK, 1)

    qb = [(q_of(u, g) * MOBA_SCALE).astype(BF16) for u, g in units]
    picked = _dot(sel_ref[...].astype(BF16), expand_ref[...])
    s = jnp.concatenate([_dot(qb[idx], pages_t(k_refs, u, g)) for idx, (u, g) in enumerate(units)], axis=0)
    bias = jnp.where(i == last, blast_ref[...], bfar_ref[...])
    s = jnp.where(picked > 0.5, s + bias, NEG_INF)

    v_t = [pages_t(v_refs, u, g) for u, g in units]

    def pv(p, keys):
        return jnp.concatenate([_dot_nt(unit_rows(p, idx), v_t[idx][:, keys]) for idx in range(len(units))], axis=0)

    _softmax_update(s, pv, m_ref, l_ref, acc_ref)

    @pl.when(i == last)
    def _():
        def new_rows(ref, u, g):
            return ref[u * ds:(u + 1) * ds, g * MOBA_HEAD_DIM:(g + 1) * MOBA_HEAD_DIM].astype(BF16)

        s_new = jnp.concatenate([_dot_nt(qb[idx], new_rows(knew_ref, u, g))
                                 for idx, (u, g) in enumerate(units)], axis=0) + bnew_ref[...]

        def pv_new(p, keys):
            return jnp.concatenate([_dot(unit_rows(p, idx), new_rows(vnew_ref, u, g))
                                    for idx, (u, g) in enumerate(units)], axis=0)

        _softmax_update(s_new, pv_new, m_ref, l_ref, acc_ref)
        out = acc_ref[...] / l_ref[...]
        for idx, (u, g) in enumerate(units):
            out_ref[g * MOBA_GROUP:(g + 1) * MOBA_GROUP, u * ds:(u + 1) * ds, :] = (
                unit_rows(out, idx).reshape(MOBA_GROUP, ds, MOBA_HEAD_DIM))


def _sample_moba(page_table, cache_kt, cache_vt, qm, blk_mean_t, km, vm, expand, bias_last, bias_far, bias_new,
                 ds, group, n_par):
    n_seq, n_pages = page_table.shape
    n_blk = blk_mean_t.shape[3]
    all_rows = n_par * MOBA_KV_HEADS * MOBA_GROUP * ds
    keys = group * PAGE_SIZE
    page = (1, 1, MOBA_KV_HEADS, MOBA_HEAD_DIM, PAGE_SIZE)
    tok = n_par * ds
    const2 = lambda a: pl.BlockSpec(a.shape, lambda s, i, pt: (0, 0))
    return pl.pallas_call(
        functools.partial(_sample_moba_kernel, group=group, ds=ds, n_par=n_par),
        grid_spec=pltpu.PrefetchScalarGridSpec(
            num_scalar_prefetch=1,
            grid=(n_seq // n_par, n_pages // group),
            in_specs=_page_specs(page, group, n_par) + _page_specs(page, group, n_par) + [
                pl.BlockSpec((MOBA_HEADS, tok, MOBA_HEAD_DIM), lambda s, i, pt: (0, s, 0)),
                pl.BlockSpec((n_par, MOBA_KV_HEADS, MOBA_HEAD_DIM, n_blk), lambda s, i, pt: (s, 0, 0, 0)),
                pl.BlockSpec((tok, KV_W), lambda s, i, pt: (s, 0)),
                pl.BlockSpec((tok, KV_W), lambda s, i, pt: (s, 0)),
                pl.BlockSpec((n_blk, keys), lambda s, i, pt: (0, i)),
                const2(bias_last), const2(bias_far), const2(bias_new),
            ],
            out_specs=pl.BlockSpec((MOBA_HEADS, tok, MOBA_HEAD_DIM), lambda s, i, pt: (0, s, 0)),
            scratch_shapes=[pltpu.VMEM((all_rows, n_blk), F32),
                            pltpu.VMEM((all_rows, 1), F32),
                            pltpu.VMEM((all_rows, 1), F32),
                            pltpu.VMEM((all_rows, MOBA_HEAD_DIM), F32)],
        ),
        out_shape=jax.ShapeDtypeStruct((MOBA_HEADS, n_seq * ds, MOBA_HEAD_DIM), F32),
        compiler_params=_params("parallel", "arbitrary"),
        name="sample_moba",
    )(page_table, *([cache_kt] * (n_par * group)), *([cache_vt] * (n_par * group)), qm, blk_mean_t, km, vm,
      expand, bias_last, bias_far, bias_new)


def _sample_mla_kernel(pt_ref, *refs, group, ds, n_par):
    n_pg = n_par * group
    c_refs, p_refs = refs[:n_pg], refs[n_pg:2 * n_pg]
    qlat_ref, qpe_ref, cnew_ref, pnew_ref, out_ref, m_ref, l_ref, acc_ref = refs[2 * n_pg:]
    i = pl.program_id(1)
    last = pl.num_programs(1) - 1
    rows = MLA_HEADS * ds
    seqs = range(n_par)
    q1 = [qlat_ref[:, u * ds:(u + 1) * ds, :].reshape(rows, MLA_KV_LORA).astype(BF16) for u in seqs]
    q2 = [qpe_ref[:, u * ds:(u + 1) * ds, :].reshape(rows, MLA_ROPE).astype(BF16) for u in seqs]

    @pl.when(i == 0)
    def _():
        _softmax_init(m_ref, l_ref, acc_ref)

    k1 = [jnp.concatenate([c_refs[u * group + k][0, 0] for k in range(group)], axis=0).astype(BF16)
          for u in seqs]
    k2_t = [jnp.concatenate([p_refs[u * group + k][0, 0] for k in range(group)], axis=1).astype(BF16)
            for u in seqs]
    s = jnp.concatenate([_dot_nt(q1[u], k1[u]) + _dot(q2[u], k2_t[u]) for u in seqs], axis=0)
    _softmax_update(s, lambda p, keys: jnp.concatenate([_dot(p[u * rows:(u + 1) * rows], k1[u][keys])
                                                        for u in seqs], axis=0),
                    m_ref, l_ref, acc_ref)

    @pl.when(i == last)
    def _():
        c_new = [cnew_ref[u * ds:(u + 1) * ds].astype(BF16) for u in seqs]
        p_new = [pnew_ref[u * ds:(u + 1) * ds].astype(BF16) for u in seqs]
        s_new = jnp.concatenate([_dot_nt(q1[u], c_new[u]) + _dot_nt(q2[u], p_new[u]) for u in seqs], axis=0)
        tq = jnp.bitwise_and(lax.broadcasted_iota(jnp.int32, s_new.shape, 0), ds - 1)
        tk = lax.broadcasted_iota(jnp.int32, s_new.shape, 1)
        _softmax_update(jnp.where(tk <= tq, s_new, NEG_INF),
                        lambda p, keys: jnp.concatenate([_dot(p[u * rows:(u + 1) * rows], c_new[u]) for u in seqs],
                                                        axis=0),
                        m_ref, l_ref, acc_ref)
        out = acc_ref[...] / l_ref[...]
        for u in seqs:
            out_ref[:, u * ds:(u + 1) * ds, :] = out[u * rows:(u + 1) * rows].reshape(MLA_HEADS, ds, MLA_KV_LORA)


def _sample_mla(page_table, cache_ckv, cache_kpe_t, qlat, qpe, ckv, kpe, ds, group, n_par):
    n_seq, n_pages = page_table.shape
    all_rows = n_par * MLA_HEADS * ds
    tok = n_par * ds
    return pl.pallas_call(
        functools.partial(_sample_mla_kernel, group=group, ds=ds, n_par=n_par),
        grid_spec=pltpu.PrefetchScalarGridSpec(
            num_scalar_prefetch=1,
            grid=(n_seq // n_par, n_pages // group),
            in_specs=_page_specs((1, 1, PAGE_SIZE, MLA_KV_LORA), group, n_par)
            + _page_specs((1, 1, MLA_ROPE, PAGE_SIZE), group, n_par) + [
                pl.BlockSpec((MLA_HEADS, tok, MLA_KV_LORA), lambda s, i, pt: (0, s, 0)),
                pl.BlockSpec((MLA_HEADS, tok, MLA_ROPE), lambda s, i, pt: (0, s, 0)),
                pl.BlockSpec((tok, MLA_KV_LORA), lambda s, i, pt: (s, 0)),
                pl.BlockSpec((tok, MLA_ROPE), lambda s, i, pt: (s, 0)),
            ],
            out_specs=pl.BlockSpec((MLA_HEADS, tok, MLA_KV_LORA), lambda s, i, pt: (0, s, 0)),
            scratch_shapes=[pltpu.VMEM((all_rows, 1), F32), pltpu.VMEM((all_rows, 1), F32),
                            pltpu.VMEM((all_rows, MLA_KV_LORA), F32)],
        ),
        out_shape=jax.ShapeDtypeStruct((MLA_HEADS, n_seq * ds, MLA_KV_LORA), F32),
        compiler_params=_params("parallel", "arbitrary"),
        name="sample_mla",
    )(page_table, *([cache_ckv] * (n_par * group)), *([cache_kpe_t] * (n_par * group)), qlat, qpe, ckv, kpe)


def _merge_kernel(omoba_ref, olat_ref, gates_ref, x_ref, wuv_ref, wbm_ref, wbl_ref, wout_ref, g_ref, b_ref,
                  h_ref, hb_ref, *, alpha):
    y_moba = None
    y_mla = None
    for h in range(MOBA_HEADS):
        t = _dot(wbm_ref[h], omoba_ref[h].astype(BF16))
        y_moba = t if y_moba is None else y_moba + t
    for h in range(MLA_HEADS):
        o = _dot(wuv_ref[h], olat_ref[h].astype(BF16)).astype(BF16)
        t = _dot(wbl_ref[h], o)
        y_mla = t if y_mla is None else y_mla + t
    merged = gates_ref[:D_MODEL] * y_moba + gates_ref[D_MODEL:] * y_mla
    mix = _dot(wout_ref[...], merged.astype(BF16))
    h = _layer_norm(alpha * x_ref[...] + mix.T, g_ref[...], b_ref[...])
    h_ref[...] = h
    hb_ref[...] = h.astype(BF16)


def _merge(omoba_t, olat_t, gates_t, x, wuv_t, wbm_t, wbl_t, wout_t, ln_g, ln_b, alpha, name):
    n = x.shape[0]
    tb = 256
    row = lambda w: pl.BlockSpec((tb, w), lambda i: (i, 0))
    full = lambda a: pl.BlockSpec(a.shape, lambda i: (0,) * a.ndim)
    return pl.pallas_call(
        functools.partial(_merge_kernel, alpha=alpha),
        grid=(n // tb,),
        in_specs=[pl.BlockSpec((MOBA_HEADS, MOBA_HEAD_DIM, tb), lambda i: (0, 0, i)),
                  pl.BlockSpec((MLA_HEADS, MLA_KV_LORA, tb), lambda i: (0, 0, i)),
                  pl.BlockSpec((2 * D_MODEL, tb), lambda i: (0, i)),
                  row(D_MODEL), full(wuv_t), full(wbm_t), full(wbl_t), full(wout_t), full(ln_g), full(ln_b)],
        out_specs=(row(D_MODEL), row(D_MODEL)),
        out_shape=(jax.ShapeDtypeStruct((n, D_MODEL), F32), jax.ShapeDtypeStruct((n, D_MODEL), BF16)),
        compiler_params=_params("parallel"),
        name=name,
    )(omoba_t, olat_t, gates_t, x, wuv_t, wbm_t, wbl_t, wout_t, ln_g, ln_b)


def _peer_pairs():
    return [(a, b) for a in range(PEER_TOPK) for b in range(PEER_TOPK) if (a + 1) * (b + 1) <= PEER_TOPK]


def _top_values(s, n):
    vals = []
    for _ in range(n):
        m = jnp.max(s, axis=0, keepdims=True)
        vals.append(m)
        s = jnp.where(s == m, KNOCKED_OUT, s)
    return vals


def _peer_select_kernel(hb_ref, wq_ref, k1_ref, k2_ref, a_ref, n_ref, b_ref, r_ref):
    q = _dot(hb_ref[...], wq_ref[...])
    half = PEER_KEY_DIM // 2
    pairs = _peer_pairs()
    for h in range(PEER_HEADS):
        q1 = q[:, h * PEER_KEY_DIM:h * PEER_KEY_DIM + half]
        q2 = q[:, h * PEER_KEY_DIM + half:(h + 1) * PEER_KEY_DIM]
        s1 = _dot_nt(k1_ref[h], q1, precision=HIGHEST)
        s2 = _dot_nt(k2_ref[h], q2, precision=HIGHEST)
        v1 = _top_values(s1, PEER_TOPK)
        v2 = _top_values(s2, PEER_TOPK)
        cand = jnp.concatenate([v1[a] + v2[b] for a, b in pairs], axis=0)
        thr = _top_values(cand, PEER_TOPK)[-1]
        top = v1[0] + v2[0]
        z = jnp.sum(jnp.where(cand >= thr, jnp.exp(cand - top), 0.0), axis=0, keepdims=True)
        count = jnp.zeros(s1.shape, F32)
        for a in range(PEER_TOPK):
            n_a = None
            for b in range(PEER_TOPK // (a + 1)):
                t = jnp.where(v1[a] + v2[b] >= thr, 1.0, 0.0)
                n_a = t if n_a is None else n_a + t
            count = jnp.where(s1 == v1[a], n_a, count)
        rank = jnp.full(s2.shape, float(PEER_TOPK), F32)
        for k in range(PEER_TOPK):
            rank = jnp.where(s2 == v2[k], float(k), rank)
        a_ref[h] = jnp.exp(s1 - v1[0]) / z
        n_ref[h] = count
        b_ref[h] = jnp.exp(s2 - v2[0]).astype(BF16)
        r_ref[h] = rank.astype(BF16)


def _peer_select(hb, wq, k1, k2, name):
    n = hb.shape[0]
    tb = 256
    full = lambda a: pl.BlockSpec(a.shape, lambda i: (0,) * a.ndim)
    sel = pl.BlockSpec((PEER_HEADS, PEER_NKEYS, tb), lambda i: (0, 0, i))
    per_first_key = jax.ShapeDtypeStruct((PEER_HEADS, PEER_NKEYS, n), F32)
    per_second_key = jax.ShapeDtypeStruct((PEER_HEADS, PEER_NKEYS, n), BF16)
    return pl.pallas_call(
        _peer_select_kernel,
        grid=(n // tb,),
        in_specs=[pl.BlockSpec((tb, D_MODEL), lambda i: (i, 0)), full(wq), full(k1), full(k2)],
        out_specs=(sel, sel, sel, sel),
        out_shape=(per_first_key, per_first_key, per_second_key, per_second_key),
        compiler_params=_params("parallel"),
        name=name,
    )(hb, wq, k1, k2)


def _peer_expert_kernel(hb_ref, h_ref, u_ref, vt_ref, a_ref, n_ref, b_ref, r_ref, g_ref, bias_ref,
                        out_ref, acc_ref, gw_ref, *, alpha, rows_per_step):
    e = pl.program_id(1)

    @pl.when(e == 0)
    def _():
        acc_ref[...] = jnp.zeros(acc_ref.shape, F32)

    sub = PEER_SUB_ROWS * PEER_NKEYS
    n_sub = rows_per_step // PEER_SUB_ROWS
    acts = [_dot_nt(u_ref[sb * sub:(sb + 1) * sub, :], hb_ref[...]) for sb in range(n_sub)]
    partial = None
    sub_rows = range(PEER_SUB_ROWS)
    heads = range(PEER_HEADS)
    for sb in range(n_sub):
        first = e * rows_per_step + sb * PEER_SUB_ROWS
        a_full = [[a_ref[h, pl.ds(first + rr, 1), :] for h in heads] for rr in sub_rows]
        n_full = [[n_ref[h, pl.ds(first + rr, 1), :] for h in heads] for rr in sub_rows]
        tile = (BF16_SUBLANES, LANES)
        for lc in range(hb_ref.shape[0] // LANES):
            lanes = slice(lc * LANES, (lc + 1) * LANES)
            a_rows = [[jnp.broadcast_to(a_full[rr][h][:, lanes], tile).astype(BF16) for h in heads] for rr in sub_rows]
            n_rows = [[jnp.broadcast_to(n_full[rr][h][:, lanes], tile).astype(BF16) for h in heads] for rr in sub_rows]
            for sc in range(PEER_NKEYS // BF16_SUBLANES):
                keys = slice(sc * BF16_SUBLANES, (sc + 1) * BF16_SUBLANES)
                w = [None for _ in sub_rows]
                for h in heads:
                    rank_t = r_ref[h, keys, lanes]
                    b_t = b_ref[h, keys, lanes]
                    for rr in sub_rows:
                        t = jnp.where(rank_t < n_rows[rr][h], a_rows[rr][h] * b_t, jnp.zeros(tile, BF16))
                        w[rr] = t if w[rr] is None else w[rr] + t
                for rr in sub_rows:
                    rows = slice((sb * PEER_SUB_ROWS + rr) * PEER_NKEYS + sc * BF16_SUBLANES,
                                 (sb * PEER_SUB_ROWS + rr) * PEER_NKEYS + (sc + 1) * BF16_SUBLANES)
                    x = acts[sb][rr * PEER_NKEYS + sc * BF16_SUBLANES:rr * PEER_NKEYS + (sc + 1) * BF16_SUBLANES, lanes]
                    gelu = 0.5 * x * (1.0 + lax.erf(x * (2.0 ** -0.5)))
                    gw_ref[rows, lanes] = w[rr] * gelu.astype(BF16)
        t = _dot(vt_ref[:, sb * sub:(sb + 1) * sub], gw_ref[sb * sub:(sb + 1) * sub, :])
        partial = t if partial is None else partial + t
    acc_ref[...] += partial

    @pl.when(e == pl.num_programs(1) - 1)
    def _():
        f = acc_ref[...].T
        out_ref[...] = _layer_norm(alpha * h_ref[...] + f, g_ref[...], bias_ref[...])


def _peer_expert(hb, h, u_b, vt_b, sel, ln_g, ln_b, alpha, name):
    n = hb.shape[0]
    tb = 512
    rows_per_step = PEER_ROWS_PER_STEP
    n_exp = u_b.shape[0]
    et = rows_per_step * PEER_NKEYS
    sel_spec = pl.BlockSpec((PEER_HEADS, PEER_NKEYS, tb), lambda i, e: (0, 0, i))
    vec = pl.BlockSpec((1, D_MODEL), lambda i, e: (0, 0))
    return pl.pallas_call(
        functools.partial(_peer_expert_kernel, alpha=alpha, rows_per_step=rows_per_step),
        grid=(n // tb, n_exp // et),
        in_specs=[pl.BlockSpec((tb, D_MODEL), lambda i, e: (i, 0)),
                  pl.BlockSpec((tb, D_MODEL), lambda i, e: (i, 0)),
                  pl.BlockSpec((et, D_MODEL), lambda i, e: (e, 0)),
                  pl.BlockSpec((D_MODEL, et), lambda i, e: (0, e)),
                  sel_spec, sel_spec, sel_spec, sel_spec, vec, vec],
        out_specs=pl.BlockSpec((tb, D_MODEL), lambda i, e: (i, 0)),
        out_shape=jax.ShapeDtypeStruct((n, D_MODEL), F32),
        scratch_shapes=[pltpu.VMEM((D_MODEL, tb), F32), pltpu.VMEM((et, tb), BF16)],
        compiler_params=_params("parallel", "arbitrary"),
        name=name,
    )(hb, h, u_b, vt_b, *sel, ln_g, ln_b)


def _rel_bucket_static(dist):
    n = np.maximum(dist, 0)
    max_exact = REL_BUCKETS // 2
    nf = np.maximum(n, 1).astype(np.float32)
    large = max_exact + (np.log(nf / max_exact) / math.log(REL_MAX_DIST / max_exact)
                         * (REL_BUCKETS - max_exact)).astype(np.int32)
    return np.where(n < max_exact, n, np.minimum(large, REL_BUCKETS - 1)).astype(np.int32)


def _bias_tile(rel_bias, dist, causal, queries_on_lanes):
    lq, lk = dist.shape
    onehot = jax.nn.one_hot(jnp.asarray(_rel_bucket_static(dist).reshape(-1)), REL_BUCKETS, dtype=F32)
    tile = _dot(onehot, rel_bias.astype(F32), precision=HIGHEST).reshape(lq, lk, MOBA_HEADS)
    if causal:
        tile = jnp.where(jnp.asarray(dist >= 0)[:, :, None], tile, NEG_INF)
    tile = jnp.transpose(tile, (2, 0, 1)).reshape(MOBA_KV_HEADS, MOBA_GROUP * lq, lk)
    return jnp.transpose(tile, (0, 2, 1)) if queries_on_lanes else tile


def _rope_table(pos):
    half = MLA_ROPE // 2
    inv_freq = ROPE_THETA ** (-jnp.arange(half, dtype=F32) / half)
    ang = pos.astype(F32)[:, None] * inv_freq[None, :]
    cos, sin = jnp.cos(ang), jnp.sin(ang)
    cos_full = jnp.tile(jnp.concatenate([cos, cos], -1), (1, MLA_HEADS))
    sin_signed = jnp.tile(jnp.concatenate([-sin, sin], -1), (1, MLA_HEADS))
    return jnp.concatenate([cos_full, sin_signed], -1)


def _swap_halves(w):
    half = w.shape[-1] // 2
    return jnp.concatenate([w[..., half:], w[..., :half]], -1)


def _largest_divisor(n, cap):
    d = min(n, cap)
    while n % d:
        d -= 1
    return d


def kernel(x_prompt, x_sample, cache_moba_k, cache_moba_v, cache_mla_ckv, cache_mla_kpe, page_table, w_in, b_gate, mla_q_norm, mla_w_uq, mla_kv_norm, mla_w_uk, mla_w_uv, rel_bias, w_br_moba, w_br_mla, w_out, ln1_g, ln1_b, peer_wq, peer_k1, peer_k2, peer_u, peer_v, ln2_g, ln2_b):
    depth = w_in.shape[0]
    assert depth == 1, "single-layer step"
    batch, seq, _ = x_prompt.shape
    n_seq, ds, _ = x_sample.shape
    n_pages = page_table.shape[1]
    past = n_pages * PAGE_SIZE
    n_prompt = batch * seq
    n_sample = n_seq * ds
    assert seq % 512 == 0 and seq // MOBA_BLOCK >= MOBA_TOPK
    assert past % MOBA_BLOCK == 0 and past // MOBA_BLOCK >= MOBA_TOPK
    assert n_prompt % 512 == 0 and n_sample % 512 == 0 and ds == 8
    assert (seq // MOBA_BLOCK) % BLOCKS_PER_STEP == 0 and n_seq % SEQS_PER_STEP == 0
    alpha = (2 * depth) ** 0.25
    layer = 0

    w = w_in[layer]
    c0 = MOBA_W + 2 * KV_W
    lat_w = MLA_Q_LORA + MLA_KV_LORA
    wq = w[:, :MOBA_W].astype(BF16)
    wkv = w[:, MOBA_W:c0].astype(BF16)
    wv_t = w[:, MOBA_W + KV_W:c0].astype(BF16).T
    kr_w = w[:, c0 + lat_w:c0 + lat_w + MLA_ROPE]
    pad = jnp.zeros((D_MODEL, 128 - 2 * MLA_ROPE), w.dtype)
    wlat = jnp.concatenate([w[:, c0:c0 + lat_w], kr_w, _swap_halves(kr_w), pad], -1).astype(BF16)
    wgate_t = w[:, c0 + lat_w + MLA_ROPE:].astype(BF16).T
    bgate_t = b_gate[layer].reshape(-1, 1)
    uq = mla_w_uq[layer]
    uq_pe = uq[..., MLA_NOPE:]
    wuq = jnp.concatenate([uq[..., :MLA_NOPE].reshape(MLA_Q_LORA, -1), uq_pe.reshape(MLA_Q_LORA, -1),
                           _swap_halves(uq_pe).reshape(MLA_Q_LORA, -1)], -1).astype(BF16)
    wuk = jnp.transpose(mla_w_uk[layer], (1, 2, 0)).astype(BF16)
    wuk_t = jnp.transpose(mla_w_uk[layer], (1, 0, 2)).astype(BF16)
    wuv_t = jnp.transpose(mla_w_uv[layer], (1, 2, 0)).astype(BF16)
    wbm_t = jnp.transpose(w_br_moba[layer].reshape(MOBA_HEADS, MOBA_HEAD_DIM, D_MODEL), (0, 2, 1)).astype(BF16)
    wbl_t = jnp.transpose(w_br_mla[layer].reshape(MLA_HEADS, MLA_V, D_MODEL), (0, 2, 1)).astype(BF16)
    wout_t = w_out[layer].astype(BF16).T
    peer_q = peer_wq[layer].astype(BF16)
    u_b = peer_u[layer].astype(BF16)
    vt_b = peer_v[layer].astype(BF16).T
    vec = lambda a: a[layer].reshape(1, -1)
    shared = (wkv, wlat, wgate_t, bgate_t, vec(mla_q_norm), vec(mla_kv_norm))

    tab_p = _rope_table(jnp.tile(jnp.arange(seq), batch))
    tab_s = _rope_table(jnp.tile(past + jnp.arange(ds), n_seq))
    qi = np.arange(Q_BLOCK)[:, None]
    ki = np.arange(MOBA_BLOCK)[None, :]
    per_block = MOBA_BLOCK // Q_BLOCK
    bias_own = jnp.stack([_bias_tile(rel_bias, o * Q_BLOCK + qi - ki, True, True) for o in range(per_block)], 1)
    bias_prev = jnp.stack([_bias_tile(rel_bias, MOBA_BLOCK + o * Q_BLOCK + qi - ki, False, True)
                           for o in range(per_block)], 1)
    far = rel_bias[REL_BUCKETS - 1].astype(F32)
    bias_far = jnp.broadcast_to(jnp.repeat(far, Q_BLOCK).reshape(MOBA_KV_HEADS, 1, 1, -1), bias_own.shape)
    bias_tiles = jnp.stack([bias_far, bias_prev, bias_own], 2)
    ti = np.arange(ds)[:, None]
    ppb = MOBA_BLOCK // PAGE_SIZE
    mean_group = _largest_divisor(n_pages // ppb, MEAN_PAGES_PER_STEP // ppb) * ppb
    group = _largest_divisor(n_pages // ppb, ATT_PAGES_PER_STEP // ppb) * ppb
    n_par = SEQS_PER_STEP
    keys = group * PAGE_SIZE
    per_seq = lambda a: jnp.tile(a.reshape(MOBA_HEADS * ds, -1), (n_par, 1))
    far_s = jnp.repeat(far, ds).reshape(MOBA_KV_HEADS, -1, 1)
    tail = _bias_tile(rel_bias, past + ti - (past - MOBA_BLOCK + ki), False, False)
    bias_last = per_seq(jnp.concatenate(
        [jnp.broadcast_to(far_s, far_s.shape[:2] + (keys - MOBA_BLOCK,)), tail], -1))
    bias_new = per_seq(_bias_tile(rel_bias, ti - np.arange(ds)[None, :], True, False))
    far_s = per_seq(far_s)
    n_blk_s = past // MOBA_BLOCK
    expand = jnp.asarray(np.repeat(np.eye(n_blk_s, dtype=np.float32), MOBA_BLOCK, axis=1), BF16)

    cache_kt = jnp.swapaxes(cache_moba_k, 3, 4)
    cache_vt = jnp.swapaxes(cache_moba_v, 3, 4)
    cache_kpe_t = jnp.swapaxes(cache_mla_kpe, 2, 3)

    xp = x_prompt.reshape(n_prompt, D_MODEL)
    (km_p, vm_p, ckv_p, kpe_p, gates_p, qm_t, kb, vb_t, kmean, ckvb, kpeb, ckv_t, qlat_t, qpe_t) = _project_prompt(
        xp, tab_p, tab_p.T, shared, wq.T, wv_t, wuq.T, wuk_t)
    kmean_g = jnp.transpose(kmean.reshape(-1, MOBA_KV_HEADS, MOBA_HEAD_DIM), (1, 0, 2))
    olat_p = _prompt_mla(qlat_t, qpe_t, ckvb, kpeb, ckv_t, batch, seq)
    omoba_p = _prompt_moba(qm_t, kb, vb_t, kmean_g, bias_tiles, batch, seq)
    merge_w = (wuv_t, wbm_t, wbl_t, wout_t, vec(ln1_g), vec(ln1_b))
    h_p, hb_p = _merge(omoba_p, olat_p, gates_p, xp, *merge_w, alpha, "prompt_merge_ln1")
    sel_p = _peer_select(hb_p, peer_q, peer_k1[layer], peer_k2[layer], "prompt_peer_select")
    y_p = _peer_expert(hb_p, h_p, u_b, vt_b, sel_p, vec(ln2_g), vec(ln2_b), alpha, "prompt_peer_experts")

    xs = x_sample.reshape(n_sample, D_MODEL)
    km_s, vm_s, ckv_s, kpe_s, gates_s, qm, qlat, qpe = _project_sample(xs, tab_s, shared, wq, wuq, wuk)
    blk_mean_t = _block_means(page_table, cache_kt, mean_group)
    omoba_s = _sample_moba(page_table, cache_kt, cache_vt, qm, blk_mean_t, km_s, vm_s, expand, bias_last,
                           far_s, bias_new, ds, group, n_par)
    olat_s = _sample_mla(page_table, cache_mla_ckv, cache_kpe_t, qlat, qpe, ckv_s, kpe_s, ds, group, n_par)
    h_s, hb_s = _merge(jnp.swapaxes(omoba_s, 1, 2), jnp.swapaxes(olat_s, 1, 2), gates_s, xs, *merge_w, alpha,
                       "sample_merge_ln1")
    sel_s = _peer_select(hb_s, peer_q, peer_k1[layer], peer_k2[layer], "sample_peer_select")
    y_s = _peer_expert(hb_s, h_s, u_b, vt_b, sel_s, vec(ln2_g), vec(ln2_b), alpha, "sample_peer_experts")

    lead_p, lead_s = (depth, batch, seq), (depth, n_seq, ds)
    kv_tail = (MOBA_KV_HEADS, MOBA_HEAD_DIM)
    return (
        y_p.reshape(batch, seq, D_MODEL),
        y_s.reshape(n_seq, ds, D_MODEL),
        km_p.reshape(lead_p + kv_tail), vm_p.reshape(lead_p + kv_tail),
        ckv_p.reshape(lead_p + (MLA_KV_LORA,)), kpe_p.reshape(lead_p + (MLA_ROPE,)),
        km_s.reshape(lead_s + kv_tail), vm_s.reshape(lead_s + kv_tail),
        ckv_s.reshape(lead_s + (MLA_KV_LORA,)), kpe_s.reshape(lead_s + (MLA_ROPE,)),
    )
```

```python
import functools
import math

import numpy as np
import jax
import jax.numpy as jnp
from jax import lax
from jax.experimental import pallas as pl
from jax.experimental.pallas import tpu as pltpu

D_MODEL = 1024
PAGE_SIZE = 128
MOBA_HEADS = 8
MOBA_KV_HEADS = 4
MOBA_GROUP = MOBA_HEADS // MOBA_KV_HEADS
MOBA_HEAD_DIM = 64
MOBA_BLOCK = 256
MOBA_TOPK = 3
BLOCKS_PER_STEP = 8
BLOCKS_PER_UPDATE = 1
MEAN_PAGES_PER_STEP = 32
ATT_PAGES_PER_STEP = 16
SEQS_PER_STEP = 2
MLA_HEADS = 8
MLA_NOPE = 64
MLA_ROPE = 32
MLA_V = 64
MLA_Q_LORA = 256
MLA_KV_LORA = 256
ROPE_THETA = 10000.0
REL_BUCKETS = 32
REL_MAX_DIST = 128
PEER_HEADS = 8
PEER_NKEYS = 128
PEER_KEY_DIM = 128
PEER_TOPK = 16
PEER_ROWS_PER_STEP = 16
PEER_SUB_ROWS = 2
Q_BLOCK = 128
LN_EPS = 1e-5
RMS_EPS = 1e-6
NEG_INF = -1e30
KNOCKED_OUT = -3e38

KV_W = MOBA_KV_HEADS * MOBA_HEAD_DIM
MOBA_W = MOBA_HEADS * MOBA_HEAD_DIM
NOPE_W = MLA_HEADS * MLA_NOPE
PE_W = MLA_HEADS * MLA_ROPE
MLA_SCALE = (MLA_NOPE + MLA_ROPE) ** -0.5
MOBA_SCALE = MOBA_HEAD_DIM ** -0.5

VMEM_LIMIT_BYTES = 56 * 1024 * 1024
SUBLANES, LANES = 8, 128
BF16_SUBLANES = 2 * SUBLANES

BF16 = jnp.bfloat16
F32 = jnp.float32
HIGHEST = lax.Precision.HIGHEST


def _dot(a, b, precision=None):
    return jnp.dot(a, b, precision=precision, preferred_element_type=F32)


def _dot_nt(a, b, precision=None):
    return lax.dot_general(a, b, (((1,), (1,)), ((), ())), precision=precision,
                           preferred_element_type=F32)


def _params(*sem):
    return pltpu.CompilerParams(dimension_semantics=sem, vmem_limit_bytes=VMEM_LIMIT_BYTES)


def _rms(x, g):
    return x * lax.rsqrt(jnp.mean(x * x, axis=-1, keepdims=True) + RMS_EPS) * g


def _layer_norm(x, g, b):
    mu = jnp.mean(x, axis=-1, keepdims=True)
    xc = x - mu
    var = jnp.mean(xc * xc, axis=-1, keepdims=True)
    return xc * lax.rsqrt(var + LN_EPS) * g + b


def _sigmoid(x):
    return 1.0 / (1.0 + jnp.exp(-x))


def _softmax_init(m_ref, l_ref, acc_ref):
    m_ref[...] = jnp.full(m_ref.shape, NEG_INF, F32)
    l_ref[...] = jnp.zeros(l_ref.shape, F32)
    acc_ref[...] = jnp.zeros(acc_ref.shape, F32)


def _softmax_update(s, pv, m_ref, l_ref, acc_ref):
    m_old = m_ref[...]
    m_new = jnp.maximum(m_old, jnp.max(s, axis=-1, keepdims=True))
    alpha = jnp.exp(m_old - m_new)
    p = jnp.exp(s - m_new)
    l_ref[...] = alpha * l_ref[...] + jnp.sum(p, axis=-1, keepdims=True)
    acc_ref[...] = alpha * acc_ref[...] + pv(p.astype(BF16), slice(None))
    m_ref[...] = m_new


def _top_mask(s, n_sel, axis):
    pos = lax.broadcasted_iota(jnp.int32, s.shape, axis)
    sel = jnp.zeros(s.shape, F32)
    for _ in range(n_sel):
        m = jnp.max(s, axis=axis, keepdims=True)
        idx = jnp.min(jnp.where(s == m, pos, s.shape[axis]), axis=axis, keepdims=True)
        pick = pos == idx
        sel = jnp.where(pick, 1.0, sel)
        s = jnp.where(pick, KNOCKED_OUT, s)
    return sel


def _proj_common(x_ref, tab_ref, wkv_ref, wlat_ref, wgate_ref, bgate_ref, qn_ref, kvn_ref,
                 km_ref, vm_ref, ckv_ref, kpe_ref, gates_ref):
    xb = x_ref[...].astype(BF16)
    kv = _dot(xb, wkv_ref[...])
    km, vm = kv[:, :KV_W], kv[:, KV_W:]
    km_ref[...] = km
    vm_ref[...] = vm
    lat = _dot(xb, wlat_ref[...])
    cq = lat[:, :MLA_Q_LORA]
    ckv = lat[:, MLA_Q_LORA:MLA_Q_LORA + MLA_KV_LORA]
    base = MLA_Q_LORA + MLA_KV_LORA
    kr = lat[:, base:base + MLA_ROPE]
    kr_swapped = lat[:, base + MLA_ROPE:base + 2 * MLA_ROPE]
    ckv_n = _rms(ckv, kvn_ref[...])
    ckv_ref[...] = ckv_n
    kpe = kr * tab_ref[:, :MLA_ROPE] + kr_swapped * tab_ref[:, PE_W:PE_W + MLA_ROPE]
    kpe_ref[...] = kpe
    gates_ref[...] = _sigmoid(_dot_nt(wgate_ref[...], xb) + bgate_ref[...])
    cqn = _rms(cq, qn_ref[...]).astype(BF16)
    return xb, km, ckv_n, kpe, cqn


def _proj_prompt_kernel(x_ref, tab_ref, tabt_ref, wkv_ref, wlat_ref, wgate_ref, bgate_ref, qn_ref, kvn_ref,
                        wqt_ref, wvt_ref, wuqt_ref, wukt_ref,
                        km_ref, vm_ref, ckv_ref, kpe_ref, gates_ref,
                        qmt_ref, kb_ref, vbt_ref, kmean_ref, ckvb_ref, kpeb_ref, ckvt_ref, qlatt_ref, qpet_ref):
    xb, km, ckv_n, kpe, cqn = _proj_common(x_ref, tab_ref, wkv_ref, wlat_ref, wgate_ref, bgate_ref, qn_ref,
                                           kvn_ref, km_ref, vm_ref, ckv_ref, kpe_ref, gates_ref)
    q_t = _dot_nt(wqt_ref[...], xb)
    for h in range(MOBA_HEADS):
        qmt_ref[h] = q_t[h * MOBA_HEAD_DIM:(h + 1) * MOBA_HEAD_DIM]
    v_t = _dot_nt(wvt_ref[...], xb)
    for g in range(MOBA_KV_HEADS):
        kb_ref[g] = km[:, g * MOBA_HEAD_DIM:(g + 1) * MOBA_HEAD_DIM].astype(BF16)
        vbt_ref[g, 0] = v_t[g * MOBA_HEAD_DIM:(g + 1) * MOBA_HEAD_DIM].astype(BF16)
    kmean_ref[0] = jnp.mean(km, axis=0, keepdims=True)
    ckvb_ref[...] = ckv_n.astype(BF16)
    kpeb_ref[...] = kpe.astype(BF16)
    ckvt_ref[0] = ckv_n.T.astype(BF16)
    qq_t = _dot_nt(wuqt_ref[...], cqn)
    qpe_t = (qq_t[NOPE_W:NOPE_W + PE_W] * tabt_ref[:PE_W] + qq_t[NOPE_W + PE_W:] * tabt_ref[PE_W:]) * MLA_SCALE
    for h in range(MLA_HEADS):
        qpet_ref[h] = qpe_t[h * MLA_ROPE:(h + 1) * MLA_ROPE].astype(BF16)
        q_nope_t = qq_t[h * MLA_NOPE:(h + 1) * MLA_NOPE].astype(BF16)
        qlatt_ref[h] = (_dot(wukt_ref[h], q_nope_t) * MLA_SCALE).astype(BF16)


def _proj_sample_kernel(x_ref, tab_ref, wkv_ref, wlat_ref, wgate_ref, bgate_ref, qn_ref, kvn_ref,
                        wq_ref, wuq_ref, wuk_ref,
                        km_ref, vm_ref, ckv_ref, kpe_ref, gates_ref, qm_ref, qlat_ref, qpe_ref):
    xb, _, _, _, cqn = _proj_common(x_ref, tab_ref, wkv_ref, wlat_ref, wgate_ref, bgate_ref, qn_ref,
                                    kvn_ref, km_ref, vm_ref, ckv_ref, kpe_ref, gates_ref)
    q = _dot(xb, wq_ref[...])
    for h in range(MOBA_HEADS):
        qm_ref[h] = q[:, h * MOBA_HEAD_DIM:(h + 1) * MOBA_HEAD_DIM]
    qq = _dot(cqn, wuq_ref[...])
    qpe = (qq[:, NOPE_W:NOPE_W + PE_W] * tab_ref[:, :PE_W] + qq[:, NOPE_W + PE_W:] * tab_ref[:, PE_W:]) * MLA_SCALE
    for h in range(MLA_HEADS):
        qpe_ref[h] = qpe[:, h * MLA_ROPE:(h + 1) * MLA_ROPE]
        q_nope = qq[:, h * MLA_NOPE:(h + 1) * MLA_NOPE].astype(BF16)
        qlat_ref[h] = _dot(q_nope, wuk_ref[h]) * MLA_SCALE


def _project(x, tab, weights, extra_in, extra_shapes, extra_specs, body, name):
    n = x.shape[0]
    tb = MOBA_BLOCK
    row = lambda w: pl.BlockSpec((tb, w), lambda i: (i, 0))
    full = lambda a: pl.BlockSpec(a.shape, lambda i: (0,) * a.ndim)
    common_shapes = (
        jax.ShapeDtypeStruct((n, KV_W), F32), jax.ShapeDtypeStruct((n, KV_W), F32),
        jax.ShapeDtypeStruct((n, MLA_KV_LORA), F32), jax.ShapeDtypeStruct((n, MLA_ROPE), F32),
        jax.ShapeDtypeStruct((2 * D_MODEL, n), F32),
    )
    common_specs = (row(KV_W), row(KV_W), row(MLA_KV_LORA), row(MLA_ROPE),
                    pl.BlockSpec((2 * D_MODEL, tb), lambda i: (0, i)))
    operands = (x,) + tuple(tab) + tuple(weights) + tuple(extra_in)
    in_specs = [row(D_MODEL), row(tab[0].shape[1])]
    if len(tab) == 2:
        in_specs.append(pl.BlockSpec((tab[1].shape[0], tb), lambda i: (0, i)))
    in_specs += [full(a) for a in tuple(weights) + tuple(extra_in)]
    return pl.pallas_call(
        body,
        grid=(n // tb,),
        in_specs=in_specs,
        out_specs=common_specs + tuple(extra_specs),
        out_shape=common_shapes + tuple(extra_shapes),
        compiler_params=_params("parallel"),
        name=name,
    )(*operands)


def _project_prompt(x, tab, tab_t, weights, wq_t, wv_t, wuq_t, wuk_t):
    n = x.shape[0]
    tb = MOBA_BLOCK
    nb = n // tb
    shapes = (
        jax.ShapeDtypeStruct((MOBA_HEADS, MOBA_HEAD_DIM, n), F32),
        jax.ShapeDtypeStruct((MOBA_KV_HEADS, n, MOBA_HEAD_DIM), BF16),
        jax.ShapeDtypeStruct((MOBA_KV_HEADS, nb, MOBA_HEAD_DIM, tb), BF16),
        jax.ShapeDtypeStruct((nb, 1, KV_W), F32),
        jax.ShapeDtypeStruct((n, MLA_KV_LORA), BF16),
        jax.ShapeDtypeStruct((n, MLA_ROPE), BF16),
        jax.ShapeDtypeStruct((nb, MLA_KV_LORA, tb), BF16),
        jax.ShapeDtypeStruct((MLA_HEADS, MLA_KV_LORA, n), BF16),
        jax.ShapeDtypeStruct((MLA_HEADS, MLA_ROPE, n), BF16),
    )
    specs = (
        pl.BlockSpec((MOBA_HEADS, MOBA_HEAD_DIM, tb), lambda i: (0, 0, i)),
        pl.BlockSpec((MOBA_KV_HEADS, tb, MOBA_HEAD_DIM), lambda i: (0, i, 0)),
        pl.BlockSpec((MOBA_KV_HEADS, 1, MOBA_HEAD_DIM, tb), lambda i: (0, i, 0, 0)),
        pl.BlockSpec((1, 1, KV_W), lambda i: (i, 0, 0)),
        pl.BlockSpec((tb, MLA_KV_LORA), lambda i: (i, 0)),
        pl.BlockSpec((tb, MLA_ROPE), lambda i: (i, 0)),
        pl.BlockSpec((1, MLA_KV_LORA, tb), lambda i: (i, 0, 0)),
        pl.BlockSpec((MLA_HEADS, MLA_KV_LORA, tb), lambda i: (0, 0, i)),
        pl.BlockSpec((MLA_HEADS, MLA_ROPE, tb), lambda i: (0, 0, i)),
    )
    return _project(x, (tab, tab_t), weights, (wq_t, wv_t, wuq_t, wuk_t), shapes, specs,
                    _proj_prompt_kernel, "prompt_projection")


def _project_sample(x, tab, weights, wq, wuq, wuk):
    n = x.shape[0]
    tb = MOBA_BLOCK
    shapes = (
        jax.ShapeDtypeStruct((MOBA_HEADS, n, MOBA_HEAD_DIM), F32),
        jax.ShapeDtypeStruct((MLA_HEADS, n, MLA_KV_LORA), F32),
        jax.ShapeDtypeStruct((MLA_HEADS, n, MLA_ROPE), F32),
    )
    head = lambda nh, w: pl.BlockSpec((nh, tb, w), lambda i: (0, i, 0))
    specs = (head(MOBA_HEADS, MOBA_HEAD_DIM), head(MLA_HEADS, MLA_KV_LORA), head(MLA_HEADS, MLA_ROPE))
    return _project(x, (tab,), weights, (wq, wuq, wuk), shapes, specs, _proj_sample_kernel, "sample_projection")


def _prompt_mla_kernel(qlat_ref, qpe_ref, ckv_ref, kpe_ref, ckvt_ref, out_ref, wq_ref, wpe_ref,
                       m_ref, l_ref, acc_ref, *, tq, tk):
    c = pl.program_id(1)
    pair_cols = 2 * tq
    n_pairs = MLA_HEADS // 2
    for p in range(n_pairs):
        wq_ref[p] = jnp.concatenate([qlat_ref[2 * p], qlat_ref[2 * p + 1]], axis=1)
        wpe_ref[p] = jnp.concatenate([qpe_ref[2 * p], qpe_ref[2 * p + 1]], axis=1)
    _softmax_init(m_ref, l_ref, acc_ref)

    def step(blk0, n_keys, masked):
        n_sub = n_keys // MOBA_BLOCK
        start = pl.multiple_of(blk0 * MOBA_BLOCK, MOBA_BLOCK)
        k1 = ckv_ref[pl.ds(start, n_keys), :]
        k2 = kpe_ref[pl.ds(start, n_keys), :]
        v_t = [ckvt_ref[blk0 + u] for u in range(n_sub)]
        scores = []
        for pr in range(n_pairs):
            s = _dot(k1, wq_ref[pr]) + _dot(k2, wpe_ref[pr])
            if masked:
                q_pos = c * tq + jnp.bitwise_and(lax.broadcasted_iota(jnp.int32, s.shape, 1), tq - 1)
                k_pos = start + lax.broadcasted_iota(jnp.int32, s.shape, 0)
                s = jnp.where(k_pos <= q_pos, s, NEG_INF)
            scores.append(s)
        m_old = m_ref[...]
        l_old = l_ref[...]
        m_new = jnp.concatenate(
            [jnp.maximum(m_old[:, pr * pair_cols:(pr + 1) * pair_cols], jnp.max(scores[pr], axis=0, keepdims=True))
             for pr in range(n_pairs)], axis=1)
        alpha = jnp.exp(m_old - m_new)
        probs = [jnp.exp(scores[pr] - m_new[:, pr * pair_cols:(pr + 1) * pair_cols]) for pr in range(n_pairs)]
        l_ref[...] = alpha * l_old + jnp.concatenate([jnp.sum(p, axis=0, keepdims=True) for p in probs], axis=1)
        m_ref[...] = m_new
        for pr in range(n_pairs):
            cols = slice(pr * pair_cols, (pr + 1) * pair_cols)
            p = probs[pr].astype(BF16)
            pv = None
            for u in range(n_sub):
                t = _dot(v_t[u], p[u * MOBA_BLOCK:(u + 1) * MOBA_BLOCK])
                pv = t if pv is None else pv + t
            acc_ref[:, cols] = alpha[:, cols] * acc_ref[:, cols] + pv

    short = tk // 2
    n_long = (c * tq) // tk
    blocks_per_long = tk // MOBA_BLOCK
    blocks_per_short = short // MOBA_BLOCK

    def body(j, carry):
        step(j * blocks_per_long, tk, False)
        return carry

    lax.fori_loop(0, n_long, body, 0)
    n_short = (c * tq) // short

    @pl.when(n_short > 2 * n_long)
    def _():
        step(2 * n_long * blocks_per_short, short, False)

    step(n_short * blocks_per_short, short, True)
    for h in range(MLA_HEADS):
        cols = slice(h * tq, (h + 1) * tq)
        out_ref[h] = acc_ref[:, cols] / l_ref[:, cols]


def _prompt_mla(qlat_t, qpe_t, ckvb, kpeb, ckv_t, batch, seq):
    tq = Q_BLOCK
    tk = min(1024, seq)
    nq = seq // tq
    cols = MLA_HEADS * tq
    n = batch * seq
    return pl.pallas_call(
        functools.partial(_prompt_mla_kernel, tq=tq, tk=tk),
        grid=(batch, nq),
        in_specs=[
            pl.BlockSpec((MLA_HEADS, MLA_KV_LORA, tq), lambda b, c: (0, 0, b * nq + c)),
            pl.BlockSpec((MLA_HEADS, MLA_ROPE, tq), lambda b, c: (0, 0, b * nq + c)),
            pl.BlockSpec((seq, MLA_KV_LORA), lambda b, c: (b, 0)),
            pl.BlockSpec((seq, MLA_ROPE), lambda b, c: (b, 0)),
            pl.BlockSpec((seq // MOBA_BLOCK, MLA_KV_LORA, MOBA_BLOCK), lambda b, c: (b, 0, 0)),
        ],
        out_specs=pl.BlockSpec((MLA_HEADS, MLA_KV_LORA, tq), lambda b, c: (0, 0, b * nq + c)),
        out_shape=jax.ShapeDtypeStruct((MLA_HEADS, MLA_KV_LORA, n), F32),
        scratch_shapes=[pltpu.VMEM((MLA_HEADS // 2, MLA_KV_LORA, 2 * tq), BF16),
                        pltpu.VMEM((MLA_HEADS // 2, MLA_ROPE, 2 * tq), BF16),
                        pltpu.VMEM((1, cols), F32), pltpu.VMEM((1, cols), F32),
                        pltpu.VMEM((MLA_KV_LORA, cols), F32)],
        compiler_params=_params("parallel", "arbitrary"),
        name="prompt_mla",
    )(qlat_t, qpe_t, ckvb, kpeb, ckv_t)


def _prompt_moba_kernel(qt_ref, kb_ref, vt_ref, kmean_ref, bias_ref, out_ref,
                        sel_ref, m_ref, l_ref, acc_ref, *, tq, n_blk):
    c = pl.program_id(2)
    cols = MOBA_GROUP * tq
    own = (c * tq) // MOBA_BLOCK
    q_t = jnp.concatenate([qt_ref[u] for u in range(MOBA_GROUP)], axis=1)
    qb_t = (q_t * MOBA_SCALE).astype(BF16)

    s_blk = _dot(kmean_ref[0], q_t, precision=HIGHEST)
    blk = lax.broadcasted_iota(jnp.int32, (n_blk, cols), 0)
    s_blk = jnp.where(blk < own, s_blk, NEG_INF)
    sel_ref[...] = jnp.where(blk < own, _top_mask(s_blk, MOBA_TOPK, 0), 0.0)

    _softmax_init(m_ref, l_ref, acc_ref)

    def group_step(jj, carry):
        first = jj * BLOCKS_PER_STEP
        start = pl.multiple_of(first * MOBA_BLOCK, BLOCKS_PER_STEP * MOBA_BLOCK)
        s_all = _dot(kb_ref[0, pl.ds(start, BLOCKS_PER_STEP * MOBA_BLOCK), :], qb_t)
        m, l, acc = m_ref[...], l_ref[...], acc_ref[...]
        for ss in range(BLOCKS_PER_STEP // BLOCKS_PER_UPDATE):
            parts = []
            for u in range(BLOCKS_PER_UPDATE):
                off = ss * BLOCKS_PER_UPDATE + u
                j = first + off
                bias = bias_ref[0, 0, jnp.clip(j - own + 2, 0, 2)]
                picked = jnp.where(j == own, 1.0, sel_ref[pl.ds(j, 1), :])
                s = s_all[off * MOBA_BLOCK:(off + 1) * MOBA_BLOCK]
                parts.append(jnp.where(picked > 0.5, s + bias, NEG_INF))
            m_new = m
            for s in parts:
                m_new = jnp.maximum(m_new, jnp.max(s, axis=0, keepdims=True))
            alpha = jnp.exp(m - m_new)
            l = alpha * l
            pv = None
            for u in range(BLOCKS_PER_UPDATE):
                p = jnp.exp(parts[u] - m_new)
                l = l + jnp.sum(p, axis=0, keepdims=True)
                t = _dot(vt_ref[0, first + ss * BLOCKS_PER_UPDATE + u], p.astype(BF16))
                pv = t if pv is None else pv + t
            acc = alpha * acc + pv
            m = m_new
        m_ref[...], l_ref[...], acc_ref[...] = m, l, acc
        return carry

    lax.fori_loop(0, own // BLOCKS_PER_STEP + 1, group_step, 0)
    out = acc_ref[...] / l_ref[...]
    for u in range(MOBA_GROUP):
        out_ref[u] = out[:, u * tq:(u + 1) * tq]


def _prompt_moba(qm_t, kb, vb_t, kmean_g, bias_tiles, batch, seq):
    tq = Q_BLOCK
    nq = seq // tq
    n_blk = seq // MOBA_BLOCK
    cols = MOBA_GROUP * tq
    per_block = MOBA_BLOCK // tq
    n = batch * seq
    return pl.pallas_call(
        functools.partial(_prompt_moba_kernel, tq=tq, n_blk=n_blk),
        grid=(batch, MOBA_KV_HEADS, nq),
        in_specs=[
            pl.BlockSpec((MOBA_GROUP, MOBA_HEAD_DIM, tq), lambda b, g, c: (g, 0, b * nq + c)),
            pl.BlockSpec((1, seq, MOBA_HEAD_DIM), lambda b, g, c: (g, b, 0)),
            pl.BlockSpec((1, n_blk, MOBA_HEAD_DIM, MOBA_BLOCK), lambda b, g, c: (g, b, 0, 0)),
            pl.BlockSpec((1, n_blk, MOBA_HEAD_DIM), lambda b, g, c: (g, b, 0)),
            pl.BlockSpec((1, 1, 3, MOBA_BLOCK, cols), lambda b, g, c: (g, c % per_block, 0, 0, 0)),
        ],
        out_specs=pl.BlockSpec((MOBA_GROUP, MOBA_HEAD_DIM, tq), lambda b, g, c: (g, 0, b * nq + c)),
        out_shape=jax.ShapeDtypeStruct((MOBA_HEADS, MOBA_HEAD_DIM, n), F32),
        scratch_shapes=[pltpu.VMEM((n_blk, cols), F32), pltpu.VMEM((1, cols), F32), pltpu.VMEM((1, cols), F32),
                        pltpu.VMEM((MOBA_HEAD_DIM, cols), F32)],
        compiler_params=_params("parallel", "parallel", "arbitrary"),
        name="prompt_moba",
    )(qm_t, kb, vb_t, kmean_g, bias_tiles)


def _page_specs(block_shape, group, n_par=1):
    nd = len(block_shape)

    def make(u, k):
        def index_map(s, i, pt):
            return (0, pt[s * n_par + u, i * group + k]) + (0,) * (nd - 2)
        return pl.BlockSpec(block_shape, index_map)

    return [make(u, k) for u in range(n_par) for k in range(group)]


def _block_mean_kernel(pt_ref, *refs, group, n_blk):
    k_refs, out_ref = refs[:group], refs[group]
    i = pl.program_id(1)
    ppb = MOBA_BLOCK // PAGE_SIZE

    @pl.when(i == 0)
    def _():
        out_ref[...] = jnp.zeros(out_ref.shape, F32)

    col = lax.broadcasted_iota(jnp.int32, (MOBA_HEAD_DIM, n_blk), 1)
    for g in range(MOBA_KV_HEADS):
        cur = out_ref[0, g]
        for jb in range(group // ppb):
            acc = k_refs[jb * ppb][0, 0, g]
            for p in range(1, ppb):
                acc = acc + k_refs[jb * ppb + p][0, 0, g]
            mean = jnp.sum(acc, axis=1, keepdims=True) * (1.0 / MOBA_BLOCK)
            cur = jnp.where(col == i * (group // ppb) + jb, mean, cur)
        out_ref[0, g] = cur


def _block_means(page_table, cache_kt, group):
    n_seq, n_pages = page_table.shape
    n_blk = n_pages // (MOBA_BLOCK // PAGE_SIZE)
    page = (1, 1, MOBA_KV_HEADS, MOBA_HEAD_DIM, PAGE_SIZE)
    return pl.pallas_call(
        functools.partial(_block_mean_kernel, group=group, n_blk=n_blk),
        grid_spec=pltpu.PrefetchScalarGridSpec(
            num_scalar_prefetch=1,
            grid=(n_seq, n_pages // group),
            in_specs=_page_specs(page, group),
            out_specs=pl.BlockSpec((1, MOBA_KV_HEADS, MOBA_HEAD_DIM, n_blk), lambda s, i, pt: (s, 0, 0, 0)),
        ),
        out_shape=jax.ShapeDtypeStruct((n_seq, MOBA_KV_HEADS, MOBA_HEAD_DIM, n_blk), F32),
        compiler_params=_params("parallel", "arbitrary"),
        name="sample_block_means",
    )(page_table, *([cache_kt] * group))


def _sample_moba_kernel(pt_ref, *refs, group, ds, n_par):
    n_pg = n_par * group
    k_refs, v_refs = refs[:n_pg], refs[n_pg:2 * n_pg]
    (qm_ref, bm_ref, knew_ref, vnew_ref, expand_ref, blast_ref, bfar_ref, bnew_ref,
     out_ref, sel_ref, m_ref, l_ref, acc_ref) = refs[2 * n_pg:]
    i = pl.program_id(1)
    last = pl.num_programs(1) - 1
    rows = MOBA_GROUP * ds
    units = [(u, g) for u in range(n_par) for g in range(MOBA_KV_HEADS)]

    def q_of(u, g):
        q = qm_ref[g * MOBA_GROUP:(g + 1) * MOBA_GROUP, u * ds:(u + 1) * ds, :]
        return q.reshape(rows, MOBA_HEAD_DIM)

    def pages_t(page_refs, u, g):
        pages = [page_refs[u * group + k][0, 0, g] for k in range(group)]
        return jnp.concatenate(pages, axis=1).astype(BF16)

    def unit_rows(x, idx):
        return x[idx * rows:(idx + 1) * rows]

    @pl.when(i == 0)
    def _():
        _softmax_init(m_ref, l_ref, acc_ref)
        for idx, (u, g) in enumerate(units):
            s_blk = _dot(q_of(u, g), bm_ref[u, g], precision=HIGHEST)
            sel_ref[idx * rows:(idx + 1) * rows] = _top_mask(s_blk, MOBA_TOPK, 1)

    qb = [(q_of(u, g) * MOBA_SCALE).astype(BF16) for u, g in units]
    picked = _dot(sel_ref[...].astype(BF16), expand_ref[...])
    s = jnp.concatenate([_dot(qb[idx], pages_t(k_refs, u, g)) for idx, (u, g) in enumerate(units)], axis=0)
    bias = jnp.where(i == last, blast_ref[...], bfar_ref[...])
    s = jnp.where(picked > 0.5, s + bias, NEG_INF)

    v_t = [pages_t(v_refs, u, g) for u, g in units]

    def pv(p, keys):
        return jnp.concatenate([_dot_nt(unit_rows(p, idx), v_t[idx][:, keys]) for idx in range(len(units))], axis=0)

    _softmax_update(s, pv, m_ref, l_ref, acc_ref)

    @pl.when(i == last)
    def _():
        def new_rows(ref, u, g):
            return ref[u * ds:(u + 1) * ds, g * MOBA_HEAD_DIM:(g + 1) * MOBA_HEAD_DIM].astype(BF16)

        s_new = jnp.concatenate([_dot_nt(qb[idx], new_rows(knew_ref, u, g))
                                 for idx, (u, g) in enumerate(units)], axis=0) + bnew_ref[...]

        def pv_new(p, keys):
            return jnp.concatenate([_dot(unit_rows(p, idx), new_rows(vnew_ref, u, g))
                                    for idx, (u, g) in enumerate(units)], axis=0)

        _softmax_update(s_new, pv_new, m_ref, l_ref, acc_ref)
        out = acc_ref[...] / l_ref[...]
        for idx, (u, g) in enumerate(units):
            out_ref[g * MOBA_GROUP:(g + 1) * MOBA_GROUP, u * ds:(u + 1) * ds, :] = (
                unit_rows(out, idx).reshape(MOBA_GROUP, ds, MOBA_HEAD_DIM))


def _sample_moba(page_table, cache_kt, cache_vt, qm, blk_mean_t, km, vm, expand, bias_last, bias_far, bias_new,
                 ds, group, n_par):
    n_seq, n_pages = page_table.shape
    n_blk = blk_mean_t.shape[3]
    all_rows = n_par * MOBA_KV_HEADS * MOBA_GROUP * ds
    keys = group * PAGE_SIZE
    page = (1, 1, MOBA_KV_HEADS, MOBA_HEAD_DIM, PAGE_SIZE)
    tok = n_par * ds
    const2 = lambda a: pl.BlockSpec(a.shape, lambda s, i, pt: (0, 0))
    return pl.pallas_call(
        functools.partial(_sample_moba_kernel, group=group, ds=ds, n_par=n_par),
        grid_spec=pltpu.PrefetchScalarGridSpec(
            num_scalar_prefetch=1,
            grid=(n_seq // n_par, n_pages // group),
            in_specs=_page_specs(page, group, n_par) + _page_specs(page, group, n_par) + [
                pl.BlockSpec((MOBA_HEADS, tok, MOBA_HEAD_DIM), lambda s, i, pt: (0, s, 0)),
                pl.BlockSpec((n_par, MOBA_KV_HEADS, MOBA_HEAD_DIM, n_blk), lambda s, i, pt: (s, 0, 0, 0)),
                pl.BlockSpec((tok, KV_W), lambda s, i, pt: (s, 0)),
                pl.BlockSpec((tok, KV_W), lambda s, i, pt: (s, 0)),
                pl.BlockSpec((n_blk, keys), lambda s, i, pt: (0, i)),
                const2(bias_last), const2(bias_far), const2(bias_new),
            ],
            out_specs=pl.BlockSpec((MOBA_HEADS, tok, MOBA_HEAD_DIM), lambda s, i, pt: (0, s, 0)),
            scratch_shapes=[pltpu.VMEM((all_rows, n_blk), F32),
                            pltpu.VMEM((all_rows, 1), F32),
                            pltpu.VMEM((all_rows, 1), F32),
                            pltpu.VMEM((all_rows, MOBA_HEAD_DIM), F32)],
        ),
        out_shape=jax.ShapeDtypeStruct((MOBA_HEADS, n_seq * ds, MOBA_HEAD_DIM), F32),
        compiler_params=_params("parallel", "arbitrary"),
        name="sample_moba",
    )(page_table, *([cache_kt] * (n_par * group)), *([cache_vt] * (n_par * group)), qm, blk_mean_t, km, vm,
      expand, bias_last, bias_far, bias_new)


def _sample_mla_kernel(pt_ref, *refs, group, ds, n_par):
    n_pg = n_par * group
    c_refs, p_refs = refs[:n_pg], refs[n_pg:2 * n_pg]
    qlat_ref, qpe_ref, cnew_ref, pnew_ref, out_ref, m_ref, l_ref, acc_ref = refs[2 * n_pg:]
    i = pl.program_id(1)
    last = pl.num_programs(1) - 1
    rows = MLA_HEADS * ds
    seqs = range(n_par)
    q1 = [qlat_ref[:, u * ds:(u + 1) * ds, :].reshape(rows, MLA_KV_LORA).astype(BF16) for u in seqs]
    q2 = [qpe_ref[:, u * ds:(u + 1) * ds, :].reshape(rows, MLA_ROPE).astype(BF16) for u in seqs]

    @pl.when(i == 0)
    def _():
        _softmax_init(m_ref, l_ref, acc_ref)

    k1 = [jnp.concatenate([c_refs[u * group + k][0, 0] for k in range(group)], axis=0).astype(BF16)
          for u in seqs]
    k2_t = [jnp.concatenate([p_refs[u * group + k][0, 0] for k in range(group)], axis=1).astype(BF16)
            for u in seqs]
    s = jnp.concatenate([_dot_nt(q1[u], k1[u]) + _dot(q2[u], k2_t[u]) for u in seqs], axis=0)
    _softmax_update(s, lambda p, keys: jnp.concatenate([_dot(p[u * rows:(u + 1) * rows], k1[u][keys])
                                                        for u in seqs], axis=0),
                    m_ref, l_ref, acc_ref)

    @pl.when(i == last)
    def _():
        c_new = [cnew_ref[u * ds:(u + 1) * ds].astype(BF16) for u in seqs]
        p_new = [pnew_ref[u * ds:(u + 1) * ds].astype(BF16) for u in seqs]
        s_new = jnp.concatenate([_dot_nt(q1[u], c_new[u]) + _dot_nt(q2[u], p_new[u]) for u in seqs], axis=0)
        tq = jnp.bitwise_and(lax.broadcasted_iota(jnp.int32, s_new.shape, 0), ds - 1)
        tk = lax.broadcasted_iota(jnp.int32, s_new.shape, 1)
        _softmax_update(jnp.where(tk <= tq, s_new, NEG_INF),
                        lambda p, keys: jnp.concatenate([_dot(p[u * rows:(u + 1) * rows], c_new[u]) for u in seqs],
                                                        axis=0),
                        m_ref, l_ref, acc_ref)
        out = acc_ref[...] / l_ref[...]
        for u in seqs:
            out_ref[:, u * ds:(u + 1) * ds, :] = out[u * rows:(u + 1) * rows].reshape(MLA_HEADS, ds, MLA_KV_LORA)


def _sample_mla(page_table, cache_ckv, cache_kpe_t, qlat, qpe, ckv, kpe, ds, group, n_par):
    n_seq, n_pages = page_table.shape
    all_rows = n_par * MLA_HEADS * ds
    tok = n_par * ds
    return pl.pallas_call(
        functools.partial(_sample_mla_kernel, group=group, ds=ds, n_par=n_par),
        grid_spec=pltpu.PrefetchScalarGridSpec(
            num_scalar_prefetch=1,
            grid=(n_seq // n_par, n_pages // group),
            in_specs=_page_specs((1, 1, PAGE_SIZE, MLA_KV_LORA), group, n_par)
            + _page_specs((1, 1, MLA_ROPE, PAGE_SIZE), group, n_par) + [
                pl.BlockSpec((MLA_HEADS, tok, MLA_KV_LORA), lambda s, i, pt: (0, s, 0)),
                pl.BlockSpec((MLA_HEADS, tok, MLA_ROPE), lambda s, i, pt: (0, s, 0)),
                pl.BlockSpec((tok, MLA_KV_LORA), lambda s, i, pt: (s, 0)),
                pl.BlockSpec((tok, MLA_ROPE), lambda s, i, pt: (s, 0)),
            ],
            out_specs=pl.BlockSpec((MLA_HEADS, tok, MLA_KV_LORA), lambda s, i, pt: (0, s, 0)),
            scratch_shapes=[pltpu.VMEM((all_rows, 1), F32), pltpu.VMEM((all_rows, 1), F32),
                            pltpu.VMEM((all_rows, MLA_KV_LORA), F32)],
        ),
        out_shape=jax.ShapeDtypeStruct((MLA_HEADS, n_seq * ds, MLA_KV_LORA), F32),
        compiler_params=_params("parallel", "arbitrary"),
        name="sample_mla",
    )(page_table, *([cache_ckv] * (n_par * group)), *([cache_kpe_t] * (n_par * group)), qlat, qpe, ckv, kpe)


def _merge_kernel(omoba_ref, olat_ref, gates_ref, x_ref, wuv_ref, wbm_ref, wbl_ref, wout_ref, g_ref, b_ref,
                  h_ref, hb_ref, *, alpha):
    y_moba = None
    y_mla = None
    for h in range(MOBA_HEADS):
        t = _dot(wbm_ref[h], omoba_ref[h].astype(BF16))
        y_moba = t if y_moba is None else y_moba + t
    for h in range(MLA_HEADS):
        o = _dot(wuv_ref[h], olat_ref[h].astype(BF16)).astype(BF16)
        t = _dot(wbl_ref[h], o)
        y_mla = t if y_mla is None else y_mla + t
    merged = gates_ref[:D_MODEL] * y_moba + gates_ref[D_MODEL:] * y_mla
    mix = _dot(wout_ref[...], merged.astype(BF16))
    h = _layer_norm(alpha * x_ref[...] + mix.T, g_ref[...], b_ref[...])
    h_ref[...] = h
    hb_ref[...] = h.astype(BF16)


def _merge(omoba_t, olat_t, gates_t, x, wuv_t, wbm_t, wbl_t, wout_t, ln_g, ln_b, alpha, name):
    n = x.shape[0]
    tb = 256
    row = lambda w: pl.BlockSpec((tb, w), lambda i: (i, 0))
    full = lambda a: pl.BlockSpec(a.shape, lambda i: (0,) * a.ndim)
    return pl.pallas_call(
        functools.partial(_merge_kernel, alpha=alpha),
        grid=(n // tb,),
        in_specs=[pl.BlockSpec((MOBA_HEADS, MOBA_HEAD_DIM, tb), lambda i: (0, 0, i)),
                  pl.BlockSpec((MLA_HEADS, MLA_KV_LORA, tb), lambda i: (0, 0, i)),
                  pl.BlockSpec((2 * D_MODEL, tb), lambda i: (0, i)),
                  row(D_MODEL), full(wuv_t), full(wbm_t), full(wbl_t), full(wout_t), full(ln_g), full(ln_b)],
        out_specs=(row(D_MODEL), row(D_MODEL)),
        out_shape=(jax.ShapeDtypeStruct((n, D_MODEL), F32), jax.ShapeDtypeStruct((n, D_MODEL), BF16)),
        compiler_params=_params("parallel"),
        name=name,
    )(omoba_t, olat_t, gates_t, x, wuv_t, wbm_t, wbl_t, wout_t, ln_g, ln_b)


def _peer_pairs():
    return [(a, b) for a in range(PEER_TOPK) for b in range(PEER_TOPK) if (a + 1) * (b + 1) <= PEER_TOPK]


def _top_values(s, n):
    vals = []
    for _ in range(n):
        m = jnp.max(s, axis=0, keepdims=True)
        vals.append(m)
        s = jnp.where(s == m, KNOCKED_OUT, s)
    return vals


def _peer_select_kernel(hb_ref, wq_ref, k1_ref, k2_ref, a_ref, n_ref, b_ref, r_ref):
    q = _dot(hb_ref[...], wq_ref[...])
    half = PEER_KEY_DIM // 2
    pairs = _peer_pairs()
    for h in range(PEER_HEADS):
        q1 = q[:, h * PEER_KEY_DIM:h * PEER_KEY_DIM + half]
        q2 = q[:, h * PEER_KEY_DIM + half:(h + 1) * PEER_KEY_DIM]
        s1 = _dot_nt(k1_ref[h], q1, precision=HIGHEST)
        s2 = _dot_nt(k2_ref[h], q2, precision=HIGHEST)
        v1 = _top_values(s1, PEER_TOPK)
        v2 = _top_values(s2, PEER_TOPK)
        cand = jnp.concatenate([v1[a] + v2[b] for a, b in pairs], axis=0)
        thr = _top_values(cand, PEER_TOPK)[-1]
        top = v1[0] + v2[0]
        z = jnp.sum(jnp.where(cand >= thr, jnp.exp(cand - top), 0.0), axis=0, keepdims=True)
        count = jnp.zeros(s1.shape, F32)
        for a in range(PEER_TOPK):
            n_a = None
            for b in range(PEER_TOPK // (a + 1)):
                t = jnp.where(v1[a] + v2[b] >= thr, 1.0, 0.0)
                n_a = t if n_a is None else n_a + t
            count = jnp.where(s1 == v1[a], n_a, count)
        rank = jnp.full(s2.shape, float(PEER_TOPK), F32)
        for k in range(PEER_TOPK):
            rank = jnp.where(s2 == v2[k], float(k), rank)
        a_ref[h] = jnp.exp(s1 - v1[0]) / z
        n_ref[h] = count
        b_ref[h] = jnp.exp(s2 - v2[0]).astype(BF16)
        r_ref[h] = rank.astype(BF16)


def _peer_select(hb, wq, k1, k2, name):
    n = hb.shape[0]
    tb = 256
    full = lambda a: pl.BlockSpec(a.shape, lambda i: (0,) * a.ndim)
    sel = pl.BlockSpec((PEER_HEADS, PEER_NKEYS, tb), lambda i: (0, 0, i))
    per_first_key = jax.ShapeDtypeStruct((PEER_HEADS, PEER_NKEYS, n), F32)
    per_second_key = jax.ShapeDtypeStruct((PEER_HEADS, PEER_NKEYS, n), BF16)
    return pl.pallas_call(
        _peer_select_kernel,
        grid=(n // tb,),
        in_specs=[pl.BlockSpec((tb, D_MODEL), lambda i: (i, 0)), full(wq), full(k1), full(k2)],
        out_specs=(sel, sel, sel, sel),
        out_shape=(per_first_key, per_first_key, per_second_key, per_second_key),
        compiler_params=_params("parallel"),
        name=name,
    )(hb, wq, k1, k2)


def _peer_expert_kernel(hb_ref, h_ref, u_ref, vt_ref, a_ref, n_ref, b_ref, r_ref, g_ref, bias_ref,
                        out_ref, acc_ref, gw_ref, *, alpha, rows_per_step):
    e = pl.program_id(1)

    @pl.when(e == 0)
    def _():
        acc_ref[...] = jnp.zeros(acc_ref.shape, F32)

    sub = PEER_SUB_ROWS * PEER_NKEYS
    n_sub = rows_per_step // PEER_SUB_ROWS
    acts = [_dot_nt(u_ref[sb * sub:(sb + 1) * sub, :], hb_ref[...]) for sb in range(n_sub)]
    partial = None
    sub_rows = range(PEER_SUB_ROWS)
    heads = range(PEER_HEADS)
    for sb in range(n_sub):
        first = e * rows_per_step + sb * PEER_SUB_ROWS
        a_full = [[a_ref[h, pl.ds(first + rr, 1), :] for h in heads] for rr in sub_rows]
        n_full = [[n_ref[h, pl.ds(first + rr, 1), :] for h in heads] for rr in sub_rows]
        tile = (BF16_SUBLANES, LANES)
        for lc in range(hb_ref.shape[0] // LANES):
            lanes = slice(lc * LANES, (lc + 1) * LANES)
            a_rows = [[jnp.broadcast_to(a_full[rr][h][:, lanes], tile).astype(BF16) for h in heads] for rr in sub_rows]
            n_rows = [[jnp.broadcast_to(n_full[rr][h][:, lanes], tile).astype(BF16) for h in heads] for rr in sub_rows]
            for sc in range(PEER_NKEYS // BF16_SUBLANES):
                keys = slice(sc * BF16_SUBLANES, (sc + 1) * BF16_SUBLANES)
                w = [None for _ in sub_rows]
                for h in heads:
                    rank_t = r_ref[h, keys, lanes]
                    b_t = b_ref[h, keys, lanes]
                    for rr in sub_rows:
                        t = jnp.where(rank_t < n_rows[rr][h], a_rows[rr][h] * b_t, jnp.zeros(tile, BF16))
                        w[rr] = t if w[rr] is None else w[rr] + t
                for rr in sub_rows:
                    rows = slice((sb * PEER_SUB_ROWS + rr) * PEER_NKEYS + sc * BF16_SUBLANES,
                                 (sb * PEER_SUB_ROWS + rr) * PEER_NKEYS + (sc + 1) * BF16_SUBLANES)
                    x = acts[sb][rr * PEER_NKEYS + sc * BF16_SUBLANES:rr * PEER_NKEYS + (sc + 1) * BF16_SUBLANES, lanes]
                    gelu = 0.5 * x * (1.0 + lax.erf(x * (2.0 ** -0.5)))
                    gw_ref[rows, lanes] = w[rr] * gelu.astype(BF16)
        t = _dot(vt_ref[:, sb * sub:(sb + 1) * sub], gw_ref[sb * sub:(sb + 1) * sub, :])
        partial = t if partial is None else partial + t
    acc_ref[...] += partial

    @pl.when(e == pl.num_programs(1) - 1)
    def _():
        f = acc_ref[...].T
        out_ref[...] = _layer_norm(alpha * h_ref[...] + f, g_ref[...], bias_ref[...])


def _peer_expert(hb, h, u_b, vt_b, sel, ln_g, ln_b, alpha, name):
    n = hb.shape[0]
    tb = 512
    rows_per_step = PEER_ROWS_PER_STEP
    n_exp = u_b.shape[0]
    et = rows_per_step * PEER_NKEYS
    sel_spec = pl.BlockSpec((PEER_HEADS, PEER_NKEYS, tb), lambda i, e: (0, 0, i))
    vec = pl.BlockSpec((1, D_MODEL), lambda i, e: (0, 0))
    return pl.pallas_call(
        functools.partial(_peer_expert_kernel, alpha=alpha, rows_per_step=rows_per_step),
        grid=(n // tb, n_exp // et),
        in_specs=[pl.BlockSpec((tb, D_MODEL), lambda i, e: (i, 0)),
                  pl.BlockSpec((tb, D_MODEL), lambda i, e: (i, 0)),
                  pl.BlockSpec((et, D_MODEL), lambda i, e: (e, 0)),
                  pl.BlockSpec((D_MODEL, et), lambda i, e: (0, e)),
                  sel_spec, sel_spec, sel_spec, sel_spec, vec, vec],
        out_specs=pl.BlockSpec((tb, D_MODEL), lambda i, e: (i, 0)),
        out_shape=jax.ShapeDtypeStruct((n, D_MODEL), F32),
        scratch_shapes=[pltpu.VMEM((D_MODEL, tb), F32), pltpu.VMEM((et, tb), BF16)],
        compiler_params=_params("parallel", "arbitrary"),
        name=name,
    )(hb, h, u_b, vt_b, *sel, ln_g, ln_b)


def _rel_bucket_static(dist):
    n = np.maximum(dist, 0)
    max_exact = REL_BUCKETS // 2
    nf = np.maximum(n, 1).astype(np.float32)
    large = max_exact + (np.log(nf / max_exact) / math.log(REL_MAX_DIST / max_exact)
                         * (REL_BUCKETS - max_exact)).astype(np.int32)
    return np.where(n < max_exact, n, np.minimum(large, REL_BUCKETS - 1)).astype(np.int32)


def _bias_tile(rel_bias, dist, causal, queries_on_lanes):
    lq, lk = dist.shape
    onehot = jax.nn.one_hot(jnp.asarray(_rel_bucket_static(dist).reshape(-1)), REL_BUCKETS, dtype=F32)
    tile = _dot(onehot, rel_bias.astype(F32), precision=HIGHEST).reshape(lq, lk, MOBA_HEADS)
    if causal:
        tile = jnp.where(jnp.asarray(dist >= 0)[:, :, None], tile, NEG_INF)
    tile = jnp.transpose(tile, (2, 0, 1)).reshape(MOBA_KV_HEADS, MOBA_GROUP * lq, lk)
    return jnp.transpose(tile, (0, 2, 1)) if queries_on_lanes else tile


def _rope_table(pos):
    half = MLA_ROPE // 2
    inv_freq = ROPE_THETA ** (-jnp.arange(half, dtype=F32) / half)
    ang = pos.astype(F32)[:, None] * inv_freq[None, :]
    cos, sin = jnp.cos(ang), jnp.sin(ang)
    cos_full = jnp.tile(jnp.concatenate([cos, cos], -1), (1, MLA_HEADS))
    sin_signed = jnp.tile(jnp.concatenate([-sin, sin], -1), (1, MLA_HEADS))
    return jnp.concatenate([cos_full, sin_signed], -1)


def _swap_halves(w):
    half = w.shape[-1] // 2
    return jnp.concatenate([w[..., half:], w[..., :half]], -1)


def _largest_divisor(n, cap):
    d = min(n, cap)
    while n % d:
        d -= 1
    return d


def kernel(x_prompt, x_sample, cache_moba_k, cache_moba_v, cache_mla_ckv, cache_mla_kpe, page_table, w_in, b_gate, mla_q_norm, mla_w_uq, mla_kv_norm, mla_w_uk, mla_w_uv, rel_bias, w_br_moba, w_br_mla, w_out, ln1_g, ln1_b, peer_wq, peer_k1, peer_k2, peer_u, peer_v, ln2_g, ln2_b):
    depth = w_in.shape[0]
    assert depth == 1, "single-layer step"
    batch, seq, _ = x_prompt.shape
    n_seq, ds, _ = x_sample.shape
    n_pages = page_table.shape[1]
    past = n_pages * PAGE_SIZE
    n_prompt = batch * seq
    n_sample = n_seq * ds
    assert seq % 512 == 0 and seq // MOBA_BLOCK >= MOBA_TOPK
    assert past % MOBA_BLOCK == 0 and past // MOBA_BLOCK >= MOBA_TOPK
    assert n_prompt % 512 == 0 and n_sample % 512 == 0 and ds == 8
    assert (seq // MOBA_BLOCK) % BLOCKS_PER_STEP == 0 and n_seq % SEQS_PER_STEP == 0
    alpha = (2 * depth) ** 0.25
    layer = 0

    w = w_in[layer]
    c0 = MOBA_W + 2 * KV_W
    lat_w = MLA_Q_LORA + MLA_KV_LORA
    wq = w[:, :MOBA_W].astype(BF16)
    wkv = w[:, MOBA_W:c0].astype(BF16)
    wv_t = w[:, MOBA_W + KV_W:c0].astype(BF16).T
    kr_w = w[:, c0 + lat_w:c0 + lat_w + MLA_ROPE]
    pad = jnp.zeros((D_MODEL, 128 - 2 * MLA_ROPE), w.dtype)
    wlat = jnp.concatenate([w[:, c0:c0 + lat_w], kr_w, _swap_halves(kr_w), pad], -1).astype(BF16)
    wgate_t = w[:, c0 + lat_w + MLA_ROPE:].astype(BF16).T
    bgate_t = b_gate[layer].reshape(-1, 1)
    uq = mla_w_uq[layer]
    uq_pe = uq[..., MLA_NOPE:]
    wuq = jnp.concatenate([uq[..., :MLA_NOPE].reshape(MLA_Q_LORA, -1), uq_pe.reshape(MLA_Q_LORA, -1),
                           _swap_halves(uq_pe).reshape(MLA_Q_LORA, -1)], -1).astype(BF16)
    wuk = jnp.transpose(mla_w_uk[layer], (1, 2, 0)).astype(BF16)
    wuk_t = jnp.transpose(mla_w_uk[layer], (1, 0, 2)).astype(BF16)
    wuv_t = jnp.transpose(mla_w_uv[layer], (1, 2, 0)).astype(BF16)
    wbm_t = jnp.transpose(w_br_moba[layer].reshape(MOBA_HEADS, MOBA_HEAD_DIM, D_MODEL), (0, 2, 1)).astype(BF16)
    wbl_t = jnp.transpose(w_br_mla[layer].reshape(MLA_HEADS, MLA_V, D_MODEL), (0, 2, 1)).astype(BF16)
    wout_t = w_out[layer].astype(BF16).T
    peer_q = peer_wq[layer].astype(BF16)
    u_b = peer_u[layer].astype(BF16)
    vt_b = peer_v[layer].astype(BF16).T
    vec = lambda a: a[layer].reshape(1, -1)
    shared = (wkv, wlat, wgate_t, bgate_t, vec(mla_q_norm), vec(mla_kv_norm))

    tab_p = _rope_table(jnp.tile(jnp.arange(seq), batch))
    tab_s = _rope_table(jnp.tile(past + jnp.arange(ds), n_seq))
    qi = np.arange(Q_BLOCK)[:, None]
    ki = np.arange(MOBA_BLOCK)[None, :]
    per_block = MOBA_BLOCK // Q_BLOCK
    bias_own = jnp.stack([_bias_tile(rel_bias, o * Q_BLOCK + qi - ki, True, True) for o in range(per_block)], 1)
    bias_prev = jnp.stack([_bias_tile(rel_bias, MOBA_BLOCK + o * Q_BLOCK + qi - ki, False, True)
                           for o in range(per_block)], 1)
    far = rel_bias[REL_BUCKETS - 1].astype(F32)
    bias_far = jnp.broadcast_to(jnp.repeat(far, Q_BLOCK).reshape(MOBA_KV_HEADS, 1, 1, -1), bias_own.shape)
    bias_tiles = jnp.stack([bias_far, bias_prev, bias_own], 2)
    ti = np.arange(ds)[:, None]
    ppb = MOBA_BLOCK // PAGE_SIZE
    mean_group = _largest_divisor(n_pages // ppb, MEAN_PAGES_PER_STEP // ppb) * ppb
    group = _largest_divisor(n_pages // ppb, ATT_PAGES_PER_STEP // ppb) * ppb
    n_par = SEQS_PER_STEP
    keys = group * PAGE_SIZE
    per_seq = lambda a: jnp.tile(a.reshape(MOBA_HEADS * ds, -1), (n_par, 1))
    far_s = jnp.repeat(far, ds).reshape(MOBA_KV_HEADS, -1, 1)
    tail = _bias_tile(rel_bias, past + ti - (past - MOBA_BLOCK + ki), False, False)
    bias_last = per_seq(jnp.concatenate(
        [jnp.broadcast_to(far_s, far_s.shape[:2] + (keys - MOBA_BLOCK,)), tail], -1))
    bias_new = per_seq(_bias_tile(rel_bias, ti - np.arange(ds)[None, :], True, False))
    far_s = per_seq(far_s)
    n_blk_s = past // MOBA_BLOCK
    expand = jnp.asarray(np.repeat(np.eye(n_blk_s, dtype=np.float32), MOBA_BLOCK, axis=1), BF16)

    cache_kt = jnp.swapaxes(cache_moba_k, 3, 4)
    cache_vt = jnp.swapaxes(cache_moba_v, 3, 4)
    cache_kpe_t = jnp.swapaxes(cache_mla_kpe, 2, 3)

    xp = x_prompt.reshape(n_prompt, D_MODEL)
    (km_p, vm_p, ckv_p, kpe_p, gates_p, qm_t, kb, vb_t, kmean, ckvb, kpeb, ckv_t, qlat_t, qpe_t) = _project_prompt(
        xp, tab_p, tab_p.T, shared, wq.T, wv_t, wuq.T, wuk_t)
    kmean_g = jnp.transpose(kmean.reshape(-1, MOBA_KV_HEADS, MOBA_HEAD_DIM), (1, 0, 2))
    olat_p = _prompt_mla(qlat_t, qpe_t, ckvb, kpeb, ckv_t, batch, seq)
    omoba_p = _prompt_moba(qm_t, kb, vb_t, kmean_g, bias_tiles, batch, seq)
    merge_w = (wuv_t, wbm_t, wbl_t, wout_t, vec(ln1_g), vec(ln1_b))
    h_p, hb_p = _merge(omoba_p, olat_p, gates_p, xp, *merge_w, alpha, "prompt_merge_ln1")
    sel_p = _peer_select(hb_p, peer_q, peer_k1[layer], peer_k2[layer], "prompt_peer_select")
    y_p = _peer_expert(hb_p, h_p, u_b, vt_b, sel_p, vec(ln2_g), vec(ln2_b), alpha, "prompt_peer_experts")

    xs = x_sample.reshape(n_sample, D_MODEL)
    km_s, vm_s, ckv_s, kpe_s, gates_s, qm, qlat, qpe = _project_sample(xs, tab_s, shared, wq, wuq, wuk)
    blk_mean_t = _block_means(page_table, cache_kt, mean_group)
    omoba_s = _sample_moba(page_table, cache_kt, cache_vt, qm, blk_mean_t, km_s, vm_s, expand, bias_last,
                           far_s, bias_new, ds, group, n_par)
    olat_s = _sample_mla(page_table, cache_mla_ckv, cache_kpe_t, qlat, qpe, ckv_s, kpe_s, ds, group, n_par)
    h_s, hb_s = _merge(jnp.swapaxes(omoba_s, 1, 2), jnp.swapaxes(olat_s, 1, 2), gates_s, xs, *merge_w, alpha,
                       "sample_merge_ln1")
    sel_s = _peer_select(hb_s, peer_q, peer_k1[layer], peer_k2[layer], "sample_peer_select")
    y_s = _peer_expert(hb_s, h_s, u_b, vt_b, sel_s, vec(ln2_g), vec(ln2_b), alpha, "sample_peer_experts")

    lead_p, lead_s = (depth, batch, seq), (depth, n_seq, ds)
    kv_tail = (MOBA_KV_HEADS, MOBA_HEAD_DIM)
    return (
        y_p.reshape(batch, seq, D_MODEL),
        y_s.reshape(n_seq, ds, D_MODEL),
        km_p.reshape(lead_p + kv_tail), vm_p.reshape(lead_p + kv_tail),
        ckv_p.reshape(lead_p + (MLA_KV_LORA,)), kpe_p.reshape(lead_p + (MLA_ROPE,)),
        km_s.reshape(lead_s + kv_tail), vm_s.reshape(lead_s + kv_tail),
        ckv_s.reshape(lead_s + (MLA_KV_LORA,)), kpe_s.reshape(lead_s + (MLA_ROPE,)),
    )
```
